```python
import math
import jax
import jax.numpy as jnp
from jax import lax
import numpy as np


D_MODEL = 1024
BATCH = 32
SEQ = 2048
DEPTH = 2

GRID_W = 64
CTX_LEN = 256

DA_HEADS = 4
DA_HEAD_DIM = 64
DA_WIDTH = DA_HEADS * 2 * DA_HEAD_DIM
Q_BLOCK = 128
ROPE_BASE = 10000.0
CV_WIDTH = 512
CV_KERNEL = 31
GLA_HEADS = 4
GLA_DK = 64
GLA_DV = 128
GLA_KW = GLA_HEADS * GLA_DK
GLA_VW = GLA_HEADS * GLA_DV
GLA_RANK = 16
GLA_NORMALIZER = 16.0
GLA_CHUNK = 64
N_BRANCH = 3
D_FF = 2816
N_EXPERTS = 8
TOP_K = 2
EXPERT_FF = 2816
N_DENSE = (DEPTH + 1) // 2
N_MOE = DEPTH // 2
EPS = 1e-6

IN_SPLITS = (DA_WIDTH, DA_WIDTH, DA_WIDTH, 2 * CV_WIDTH, GLA_KW, GLA_KW, GLA_VW, GLA_VW, 2 * GLA_RANK, N_BRANCH * D_MODEL)
IN_WIDTH = 3 * DA_WIDTH + 2 * CV_WIDTH + 2 * GLA_KW + 2 * GLA_VW + 2 * GLA_RANK + N_BRANCH * D_MODEL

kernel_name = 'hybrid_diffattn_conformer_gla_moe_block'


def rms_norm(x, g):
    xf = x.astype(jnp.float32)
    y = xf * lax.rsqrt(jnp.mean(xf * xf, axis=-1, keepdims=True) + EPS)
    return (y * g.astype(jnp.float32)).astype(x.dtype)


def layer_norm(x, g, b):
    xf = x.astype(jnp.float32)
    mu = jnp.mean(xf, axis=-1, keepdims=True)
    xc = xf - mu
    var = jnp.mean(xc * xc, axis=-1, keepdims=True)
    return (xc * lax.rsqrt(var + EPS) * g.astype(jnp.float32) + b.astype(jnp.float32)).astype(x.dtype)


def modulate(xn, shift, scale):
    return xn * (1.0 + scale) + shift


def split_in(z):
    out = []
    off = 0
    for w in IN_SPLITS:
        out.append(z[..., off:off + w])
        off += w
    return out


def axial_rope(L):
    rows = L // GRID_W
    row = jnp.repeat(jnp.arange(rows), GRID_W)
    col = jnp.tile(jnp.arange(GRID_W), rows)
    n_freq = DA_HEAD_DIM // 4
    inv = ROPE_BASE ** (-jnp.arange(n_freq, dtype=jnp.float32) / n_freq)
    ang = jnp.stack([row, col], axis=-1).astype(jnp.float32)[:, :, None] * inv
    return jnp.cos(ang), jnp.sin(ang)


def apply_rope(x, cos, sin):
    B, L, H, C, d = x.shape
    xr = x.reshape(B, L, H, C, 2, 2, d // 4).astype(jnp.float32)
    x1, x2 = xr[..., 0, :], xr[..., 1, :]
    cs, sn = cos[:, None, None], sin[:, None, None]
    out = jnp.stack([x1 * cs - x2 * sn, x2 * cs + x1 * sn], axis=-2)
    return out.reshape(B, L, H, C, d).astype(x.dtype)


def diff_attention(q, k, v, lam, subln_g, lambda_init):
    B, Lq, H, _, d = q.shape
    nb = Lq // Q_BLOCK
    scale = d ** -0.5
    qb = jnp.moveaxis(q.reshape(B, nb, Q_BLOCK, H, 2, d), 1, 0)

    def one_block(qi):
        s = jnp.einsum('bqhcd,bkhcd->bhcqk', qi, k, preferred_element_type=jnp.float32) * scale
        p = jax.nn.softmax(s, axis=-1)
        pd = p[:, :, 0] - lam * p[:, :, 1]
        return jnp.einsum('bhqk,bkhe->bqhe', pd.astype(v.dtype), v)

    o = lax.map(one_block, qb)
    o = jnp.moveaxis(o, 0, 1).reshape(B, Lq, H, 2 * d)
    o = rms_norm(o, subln_g) * (1.0 - lambda_init)
    return o.reshape(B, Lq, H * 2 * d)


def conformer_conv(u, dw, dw_b, ln_g, ln_b, proj):
    a, g = jnp.split(u, 2, axis=-1)
    h = a * jax.nn.sigmoid(g)
    h = lax.conv_general_dilated(h, dw[:, None, :].astype(h.dtype), window_strides=(1,),
                                 padding=[(CV_KERNEL // 2, CV_KERNEL // 2)],
                                 dimension_numbers=('NWC', 'WIO', 'NWC'),
                                 feature_group_count=CV_WIDTH) + dw_b
    h = jax.nn.silu(layer_norm(h, ln_g, ln_b))
    return h @ proj


def gla_direction(q, k, v, log_a, s0):
    B, L, H, dk = q.shape
    dv = v.shape[-1]
    n = L // GLA_CHUNK
    f32 = jnp.float32
    qc = q.reshape(B, n, GLA_CHUNK, H, dk).astype(f32)
    kc = k.reshape(B, n, GLA_CHUNK, H, dk).astype(f32)
    vc = v.reshape(B, n, GLA_CHUNK, H, dv).astype(f32)
    b = jnp.cumsum(log_a.reshape(B, n, GLA_CHUNK, H, dk).astype(f32), axis=2)
    b_last = b[:, :, -1:]
    q_dec = qc * jnp.exp(b)
    k_inv = kc * jnp.exp(-b)
    scores = jnp.einsum('bnihd,bnjhd->bnhij', q_dec, k_inv)
    lower = jnp.tril(jnp.ones((GLA_CHUNK, GLA_CHUNK), dtype=bool))
    scores = jnp.where(lower, scores, 0.0)
    o_intra = jnp.einsum('bnhij,bnjhe->bnihe', scores, vc)
    k_tail = kc * jnp.exp(b_last - b)
    ds = jnp.einsum('bnjhd,bnjhe->nbhde', k_tail, vc)
    decay = jnp.moveaxis(jnp.exp(b_last[:, :, 0]), 1, 0)

    def step(s, inp):
        dcy, d_s = inp
        return dcy[..., None] * s + d_s, s

    s_final, s_in = lax.scan(step, s0.astype(f32), (decay, ds))
    o_inter = jnp.einsum('bnihd,nbhde->bnihe', q_dec, s_in)
    return (o_intra + o_inter).reshape(B, L, H, dv), s_final


def gla_inputs(gq, gk, gv, ga, a_up, a_b):
    B, L, _ = gq.shape
    q = gq.reshape(B, L, GLA_HEADS, GLA_DK) * (GLA_DK ** -0.5)
    k = gk.reshape(B, L, GLA_HEADS, GLA_DK)
    v = gv.reshape(B, L, GLA_HEADS, GLA_DV)
    la_f = jax.nn.log_sigmoid((ga[..., :GLA_RANK] @ a_up[0] + a_b[0]).astype(jnp.float32)) / GLA_NORMALIZER
    la_b = jax.nn.log_sigmoid((ga[..., GLA_RANK:] @ a_up[1] + a_b[1]).astype(jnp.float32)) / GLA_NORMALIZER
    return q, k, v, la_f.reshape(B, L, GLA_HEADS, GLA_DK), la_b.reshape(B, L, GLA_HEADS, GLA_DK)


def gla_bidirectional(q, k, v, la_f, la_b, s0_f, s0_b):
    o_f, s_f = gla_direction(q, k, v, la_f, s0_f)
    o_b, s_b = gla_direction(jnp.flip(q, 1), jnp.flip(k, 1), jnp.flip(v, 1), jnp.flip(la_b, 1), s0_b)
    return o_f + jnp.flip(o_b, 1), s_f, s_b


def gla_readout(o, gg, norm_g, proj, dtype):
    B, L = o.shape[:2]
    o = rms_norm(o, norm_g) * jax.nn.silu(gg.reshape(B, L, GLA_HEADS, GLA_DV).astype(jnp.float32))
    return o.astype(dtype).reshape(B, L, GLA_VW) @ proj


def gated_merge(gates, ya, yb, yc, w_out):
    g = jax.nn.sigmoid(gates.reshape(*gates.shape[:-1], N_BRANCH, D_MODEL))
    return (g[..., 0, :] * ya + g[..., 1, :] * yb + g[..., 2, :] * yc) @ w_out


def token_mixer(h, hc, lambda_init, need_ctx, rope_cos, rope_sin, w_in, da_lam, da_subln, da_proj,
                cv_dw, cv_dw_b, cv_ln_g, cv_ln_b, cv_proj, gla_a_up, gla_a_b, gla_norm, gla_proj, w_out):
    B, L, _ = h.shape
    Lc = hc.shape[1]
    aq, ak, av, cin, gq, gk, gv, gg, ga, gates = split_in(h @ w_in)
    aqc, akc, avc, cinc, gqc, gkc, gvc, ggc, gac, gatesc = split_in(hc @ w_in)

    lp = da_lam.astype(jnp.float32)
    lam = jnp.exp(jnp.sum(lp[0] * lp[1])) - jnp.exp(jnp.sum(lp[2] * lp[3])) + lambda_init
    q = apply_rope(aq.reshape(B, L, DA_HEADS, 2, DA_HEAD_DIM), rope_cos, rope_sin)
    k = apply_rope(ak.reshape(B, L, DA_HEADS, 2, DA_HEAD_DIM), rope_cos, rope_sin)
    v = av.reshape(B, L, DA_HEADS, 2 * DA_HEAD_DIM)
    kc = akc.reshape(B, Lc, DA_HEADS, 2, DA_HEAD_DIM)
    vc = avc.reshape(B, Lc, DA_HEADS, 2 * DA_HEAD_DIM)
    k_all = jnp.concatenate([k, kc], axis=1)
    v_all = jnp.concatenate([v, vc], axis=1)
    y_a = diff_attention(q, k_all, v_all, lam, da_subln, lambda_init) @ da_proj

    y_b = conformer_conv(cin, cv_dw, cv_dw_b, cv_ln_g, cv_ln_b, cv_proj)

    zero_s = jnp.zeros((B, GLA_HEADS, GLA_DK, GLA_DV), jnp.float32)
    qgc, kgc, vgc, lafc, labc = gla_inputs(gqc, gkc, gvc, gac, gla_a_up, gla_a_b)
    o_gc, s_ctx_f, s_ctx_b = gla_bidirectional(qgc, kgc, vgc, lafc, labc, zero_s, zero_s)
    qg, kg, vg, laf, lab = gla_inputs(gq, gk, gv, ga, gla_a_up, gla_a_b)
    o_g, _, _ = gla_bidirectional(qg, kg, vg, laf, lab, s_ctx_f, s_ctx_b)
    y_c = gla_readout(o_g, gg, gla_norm, gla_proj, h.dtype)

    y = gated_merge(gates, y_a, y_b, y_c, w_out)
    if not need_ctx:
        return y, None
    qc = aqc.reshape(B, Lc, DA_HEADS, 2, DA_HEAD_DIM)
    y_ac = diff_attention(qc, kc, vc, lam, da_subln, lambda_init) @ da_proj
    y_bc = conformer_conv(cinc, cv_dw, cv_dw_b, cv_ln_g, cv_ln_b, cv_proj)
    y_cc = gla_readout(o_gc, ggc, gla_norm, gla_proj, hc.dtype)
    yc = gated_merge(gatesc, y_ac, y_bc, y_cc, w_out)
    return y, yc


def swiglu(x, w1, w3, w2):
    return (jax.nn.silu(x @ w1) * (x @ w3)) @ w2


def moe_swiglu(x, router, w1, w3, w2):
    logits = (x @ router).astype(jnp.float32)
    top_v, top_i = lax.top_k(logits, TOP_K)
    wts = jax.nn.softmax(top_v, axis=-1)
    gate = jnp.sum(jax.nn.one_hot(top_i, N_EXPERTS, dtype=jnp.float32) * wts[..., None], axis=-2)
    out = jnp.zeros(x.shape[:-1] + (w2.shape[-1],), x.dtype)
    for e in range(N_EXPERTS):
        out = out + gate[..., e:e + 1].astype(x.dtype) * swiglu(x, w1[e], w3[e], w2[e])
    return out


def channel_mixer(h, layer, ffn_w1, ffn_w3, ffn_w2, moe_router, moe_w1, moe_w3, moe_w2):
    i = layer // 2
    if layer % 2 == 0:
        return swiglu(h, ffn_w1[i], ffn_w3[i], ffn_w2[i])
    return moe_swiglu(h, moe_router[i], moe_w1[i], moe_w3[i], moe_w2[i])


def setup_inputs(seed: int = 0) -> dict:
    key = jax.random.key(seed)
    ks = iter(jax.random.split(key, 40))
    f32 = jnp.float32

    def nrm(shape, scale):
        return jax.random.normal(next(ks), shape, f32) * scale

    def gain(shape):
        return 1.0 + nrm(shape, 0.05)

    D = D_MODEL
    return {
        'x': nrm((BATCH, SEQ, D), 1.0),
        'c': nrm((BATCH, D), 1.0),
        'ctx': nrm((BATCH, CTX_LEN, D), 1.0),
        'c_ctx': nrm((D,), 1.0),
        'ada_w': nrm((DEPTH, D, 6 * D), 0.5 * D ** -0.5),
        'ada_b': nrm((DEPTH, 6 * D), 0.01),
        'g_mix_pre': gain((DEPTH, D)),
        'g_mix_post': gain((DEPTH, D)),
        'g_ffn_pre': gain((DEPTH, D)),
        'g_ffn_post': gain((DEPTH, D)),
        'w_in': nrm((DEPTH, D, IN_WIDTH), D ** -0.5),
        'da_lambda': nrm((DEPTH, 4, DA_HEAD_DIM), 0.1),
        'da_subln': gain((DEPTH, 2 * DA_HEAD_DIM)),
        'da_proj': nrm((DEPTH, DA_WIDTH, D), DA_WIDTH ** -0.5),
        'cv_dw': nrm((DEPTH, CV_KERNEL, CV_WIDTH), CV_KERNEL ** -0.5),
        'cv_dw_b': nrm((DEPTH, CV_WIDTH), 0.01),
        'cv_ln_g': gain((DEPTH, CV_WIDTH)),
        'cv_ln_b': nrm((DEPTH, CV_WIDTH), 0.01),
        'cv_proj': nrm((DEPTH, CV_WIDTH, D), CV_WIDTH ** -0.5),
        'gla_a_up': nrm((DEPTH, 2, GLA_RANK, GLA_KW), GLA_RANK ** -0.5),
        'gla_a_b': nrm((DEPTH, 2, GLA_KW), 0.1),
        'gla_norm': gain((DEPTH, GLA_DV)),
        'gla_proj': nrm((DEPTH, GLA_VW, D), GLA_VW ** -0.5),
        'w_out': nrm((DEPTH, D, D), D ** -0.5),
        'ffn_w1': nrm((N_DENSE, D, D_FF), D ** -0.5),
        'ffn_w3': nrm((N_DENSE, D, D_FF), D ** -0.5),
        'ffn_w2': nrm((N_DENSE, D_FF, D), D_FF ** -0.5),
        'moe_router': nrm((N_MOE, D, N_EXPERTS), D ** -0.5),
        'moe_w1': nrm((N_MOE, N_EXPERTS, D, EXPERT_FF), D ** -0.5),
        'moe_w3': nrm((N_MOE, N_EXPERTS, D, EXPERT_FF), D ** -0.5),
        'moe_w2': nrm((N_MOE, N_EXPERTS, EXPERT_FF, D), EXPERT_FF ** -0.5),
    }


def reference(x, c, ctx, c_ctx, ada_w, ada_b, g_mix_pre, g_mix_post, g_ffn_pre, g_ffn_post, w_in,
              da_lambda, da_subln, da_proj, cv_dw, cv_dw_b, cv_ln_g, cv_ln_b, cv_proj, gla_a_up, gla_a_b,
              gla_norm, gla_proj, w_out, ffn_w1, ffn_w3, ffn_w2, moe_router, moe_w1, moe_w3, moe_w2):
    L = x.shape[1]
    rope_cos, rope_sin = axial_rope(L)
    xc = ctx
    silu_c = jax.nn.silu(c)[:, None, :]
    silu_cc = jax.nn.silu(c_ctx)[None, None, :]
    for l in range(DEPTH):
        need_ctx = l < DEPTH - 1
        lambda_init = 0.8 - 0.6 * math.exp(-0.3 * l)
        mod = jnp.split(silu_c @ ada_w[l] + ada_b[l], 6, axis=-1)
        modc = jnp.split(silu_cc @ ada_w[l] + ada_b[l], 6, axis=-1)
        h = modulate(rms_norm(x, g_mix_pre[l]), mod[0], mod[1])
        hc = modulate(rms_norm(xc, g_mix_pre[l]), modc[0], modc[1])
        y, yc = token_mixer(h, hc, lambda_init, need_ctx, rope_cos, rope_sin, w_in[l], da_lambda[l],
                            da_subln[l], da_proj[l], cv_dw[l], cv_dw_b[l], cv_ln_g[l], cv_ln_b[l],
                            cv_proj[l], gla_a_up[l], gla_a_b[l], gla_norm[l], gla_proj[l], w_out[l])
        x = x + mod[2] * rms_norm(y, g_mix_post[l])
        h = modulate(rms_norm(x, g_ffn_pre[l]), mod[3], mod[4])
        f = channel_mixer(h, l, ffn_w1, ffn_w3, ffn_w2, moe_router, moe_w1, moe_w3, moe_w2)
        x = x + mod[5] * rms_norm(f, g_ffn_post[l])
        if need_ctx:
            xc = xc + modc[2] * rms_norm(yc, g_mix_post[l])
            hc = modulate(rms_norm(xc, g_ffn_pre[l]), modc[3], modc[4])
            fc = channel_mixer(hc, l, ffn_w1, ffn_w3, ffn_w2, moe_router, moe_w1, moe_w3, moe_w2)
            xc = xc + modc[5] * rms_norm(fc, g_ffn_post[l])
    return x
```

```python
import functools
import math

import jax
import jax.numpy as jnp
from jax import lax
from jax.experimental import pallas as pl
from jax.experimental.pallas import tpu as pltpu

F32 = jnp.float32
BF16 = jnp.bfloat16
HIGHEST = lax.Precision.HIGHEST

D_MODEL = 1024
GRID_W = 64
DA_HEADS = 4
DA_HEAD_DIM = 64
DA_WIDTH = 512
ROPE_BASE = 10000.0
CV_WIDTH = 512
CV_KERNEL = 31
GLA_HEADS = 4
GLA_DK = 64
GLA_DV = 128
GLA_KW = 256
GLA_VW = 512
GLA_RANK = 16
GLA_NORMALIZER = 16.0
GLA_CHUNK = 64
N_EXPERTS = 8
D_FF = 2816
EPS = 1e-6
LOG2E = 1.4426950408889634

V7X_VMEM_BYTES = 64 * 1024 * 1024
VMEM_LIMIT = V7X_VMEM_BYTES - 8 * 1024 * 1024
LANES = 128
MXU_COLS = 256

Z_WIDTH = 7168
ZB_Q, ZB_K, ZB_V, ZB_GV = 0, 1, 2, 3
ZB_CIN = 2
ZB_GG = 6
ZB_GQ, ZB_GK = 14, 15
ZB_GATES = 4
MOD_ROWS = 40


def _cparams(sem, vmem=VMEM_LIMIT):
    return pltpu.CompilerParams(dimension_semantics=sem, vmem_limit_bytes=vmem)


def _sigmoid(x):
    return 1.0 / (1.0 + jnp.exp(-x))


def _rms(x, g):
    return x * lax.rsqrt(jnp.mean(x * x, axis=-1, keepdims=True) + EPS) * g


def _mod_kernel(c_ref, w_ref, b_ref, o_ref):
    c = c_ref[...]
    s = c * _sigmoid(c)
    o_ref[...] = jnp.dot(s, w_ref[...], preferred_element_type=F32, precision=HIGHEST) + b_ref[...]


def _mod_call(cc, ada_w, ada_b):
    depth, d, n6 = ada_w.shape
    tn = 1536
    return pl.pallas_call(
        _mod_kernel,
        grid=(depth, n6 // tn),
        in_specs=[pl.BlockSpec((MOD_ROWS, d), lambda l, j: (0, 0)),
                  pl.BlockSpec((None, d, tn), lambda l, j: (l, 0, j)),
                  pl.BlockSpec((None, 1, tn), lambda l, j: (l, 0, j))],
        out_specs=pl.BlockSpec((None, MOD_ROWS, tn), lambda l, j: (l, 0, j)),
        out_shape=jax.ShapeDtypeStruct((depth, MOD_ROWS, n6), F32),
        compiler_params=_cparams(("arbitrary", "arbitrary")),
        name="mod",
    )(cc, ada_w, ada_b.reshape(depth, 1, n6))


def _swap16(x):
    lane = lax.broadcasted_iota(jnp.int32, x.shape, 1)
    return jnp.where((lane & 16) == 0, pltpu.roll(x, LANES - 16, 1), pltpu.roll(x, 16, 1))


def _inproj_kernel(*refs, rope):
    if rope:
        (x_ref, g_ref, sh_ref, sc_ref, w_ref, wga_ref, cq_ref, sq_ref, ck_ref, sk_ref,
         z_ref, ga_ref, h_scr) = refs
    else:
        x_ref, g_ref, sh_ref, sc_ref, w_ref, wga_ref, z_ref, ga_ref, h_scr = refs
    j = pl.program_id(1)

    @pl.when(j == 0)
    def _():
        h = _rms(x_ref[...], g_ref[...]) * (1.0 + sc_ref[...]) + sh_ref[...]
        hb = h.astype(BF16)
        h_scr[...] = hb
        ga_ref[...] = jnp.dot(hb, wga_ref[...], preferred_element_type=F32).astype(BF16)

    acc = jnp.dot(h_scr[...], w_ref[...], preferred_element_type=F32)

    @pl.when(j == 0)
    def _():
        qscale = DA_HEAD_DIM ** -0.5 * LOG2E
        if rope:
            cols = []
            for c in range(acc.shape[1] // LANES):
                xs = acc[:, c * LANES:(c + 1) * LANES]
                is_q = c < DA_WIDTH // LANES
                cos = cq_ref[...] if is_q else ck_ref[...]
                sin = sq_ref[...] if is_q else sk_ref[...]
                cols.append((xs * cos + _swap16(xs) * sin).astype(BF16))
            z_ref[...] = jnp.concatenate(cols, axis=1)
        else:
            z_ref[:, :DA_WIDTH] = (acc[:, :DA_WIDTH] * qscale).astype(BF16)
            z_ref[:, DA_WIDTH:] = acc[:, DA_WIDTH:].astype(BF16)

    @pl.when(j != 0)
    def _():
        z_ref[...] = acc.astype(BF16)


def _inproj_call(x2, g, mod3, w_main, w_ga, rope_tabs, rows_per_mod, mod_row0, tm):
    t, d = x2.shape
    tn = 1024
    rope = rope_tabs is not None
    mrow = lambda i, j: (mod_row0 + i // rows_per_mod, 0, 0)
    in_specs = [pl.BlockSpec((tm, d), lambda i, j: (i, 0)),
                pl.BlockSpec((1, d), lambda i, j: (0, 0)),
                pl.BlockSpec((None, 1, d), mrow),
                pl.BlockSpec((None, 1, d), lambda i, j: (mod_row0 + i // rows_per_mod, 0, 1)),
                pl.BlockSpec((d, tn), lambda i, j: (0, j)),
                pl.BlockSpec((d, LANES), lambda i, j: (0, 0))]
    args = [x2, g.reshape(1, d), mod3, mod3, w_main, w_ga]
    if rope:
        nseq = rope_tabs[0].shape[0] // tm
        for tab in rope_tabs:
            in_specs.append(pl.BlockSpec((tm, LANES), lambda i, j: (i % nseq, 0)))
            args.append(tab)
    return pl.pallas_call(
        functools.partial(_inproj_kernel, rope=rope),
        grid=(t // tm, Z_WIDTH // tn),
        in_specs=in_specs,
        out_specs=[pl.BlockSpec((tm, tn), lambda i, j: (i, j)),
                   pl.BlockSpec((tm, LANES), lambda i, j: (i, 0))],
        out_shape=[jax.ShapeDtypeStruct((t, Z_WIDTH), BF16),
                   jax.ShapeDtypeStruct((t, LANES), BF16)],
        scratch_shapes=[pltpu.VMEM((tm, d), BF16)],
        compiler_params=_cparams(("arbitrary", "arbitrary")),
        name="inproj_rope" if rope else "inproj",
    )(*args)


def _rope_tables(seq):
    t = jnp.arange(seq)
    nf = DA_HEAD_DIM // 4
    inv = ROPE_BASE ** (-jnp.arange(nf, dtype=F32) / nf)
    ang_r = (t // GRID_W).astype(F32)[:, None] * inv
    ang_c = (t % GRID_W).astype(F32)[:, None] * inv
    cos64 = jnp.concatenate([jnp.cos(ang_r), jnp.cos(ang_r), jnp.cos(ang_c), jnp.cos(ang_c)], axis=1)
    sin64 = jnp.concatenate([-jnp.sin(ang_r), jnp.sin(ang_r), -jnp.sin(ang_c), jnp.sin(ang_c)], axis=1)
    cos = jnp.tile(cos64, (1, LANES // DA_HEAD_DIM))
    sin = jnp.tile(sin64, (1, LANES // DA_HEAD_DIM))
    qscale = DA_HEAD_DIM ** -0.5 * LOG2E
    return cos * qscale, sin * qscale, cos, sin


def _attn_kernel(*refs, n_src, lambda_init):
    lam_ref, subln_ref, q_ref = refs[:3]
    kv_refs = refs[3:3 + 2 * n_src]
    o_ref = refs[3 + 2 * n_src]
    tq = q_ref.shape[0]
    hw = 2 * DA_HEAD_DIM

    lp = lam_ref[...]
    lam = (jnp.exp(jnp.sum(lp[0:1] * lp[1:2], axis=-1, keepdims=True))
           - jnp.exp(jnp.sum(lp[2:3] * lp[3:4], axis=-1, keepdims=True)) + lambda_init)
    lane = lax.broadcasted_iota(jnp.int32, (tq, hw), 1)
    first = lane < DA_HEAD_DIM
    is_c0 = lax.broadcasted_iota(jnp.int32, (2 * tq, 1), 0) < tq

    for h in range(DA_HEADS):
        qh = q_ref[:, h * hw:(h + 1) * hw]
        zero = jnp.zeros_like(qh)
        qq = jnp.concatenate([jnp.where(first, qh, zero), jnp.where(first, zero, qh)], axis=0)
        scores = []
        for s in range(n_src):
            kh = kv_refs[2 * s][:, h * hw:(h + 1) * hw]
            scores.append(lax.dot_general(qq, kh, (((1,), (1,)), ((), ())),
                                          preferred_element_type=F32))
        m = scores[0].max(axis=-1, keepdims=True)
        for sc in scores[1:]:
            m = jnp.maximum(m, sc.max(axis=-1, keepdims=True))
        probs = [jnp.exp2(sc - m) for sc in scores]
        denom = probs[0].sum(axis=-1, keepdims=True)
        for p in probs[1:]:
            denom = denom + p.sum(axis=-1, keepdims=True)
        r = 1.0 / denom
        coef = jnp.where(is_c0, r, -lam * r)
        o = jnp.zeros((tq, hw), F32)
        for s in range(n_src):
            pn = probs[s] * coef
            pd = (pn[:tq] + pn[tq:]).astype(BF16)
            vh = kv_refs[2 * s + 1][:, h * hw:(h + 1) * hw]
            o = o + jnp.dot(pd, vh, preferred_element_type=F32)
        o = _rms(o, subln_ref[...]) * (1.0 - lambda_init)
        o_ref[:, h * hw:(h + 1) * hw] = o.astype(BF16)


def _attn_call(da_lambda, subln, zq3, kv_srcs, lambda_init, tq):
    b, lq, _ = zq3.shape
    in_specs = [pl.BlockSpec((4, DA_HEAD_DIM), lambda bi, i: (0, 0)),
                pl.BlockSpec((1, 2 * DA_HEAD_DIM), lambda bi, i: (0, 0)),
                pl.BlockSpec((None, tq, DA_WIDTH), lambda bi, i: (bi, i, ZB_Q))]
    args = [da_lambda, subln.reshape(1, -1), zq3]
    for src in kv_srcs:
        lk = src.shape[1]
        in_specs.append(pl.BlockSpec((None, lk, DA_WIDTH), lambda bi, i: (bi, 0, ZB_K)))
        in_specs.append(pl.BlockSpec((None, lk, DA_WIDTH), lambda bi, i: (bi, 0, ZB_V)))
        args += [src, src]
    return pl.pallas_call(
        functools.partial(_attn_kernel, n_src=len(kv_srcs), lambda_init=lambda_init),
        grid=(b, lq // tq),
        in_specs=in_specs,
        out_specs=pl.BlockSpec((None, tq, DA_WIDTH), lambda bi, i: (bi, i, 0)),
        out_shape=jax.ShapeDtypeStruct((b, lq, DA_WIDTH), BF16),
        compiler_params=_cparams(("arbitrary", "arbitrary")),
        name="attn",
    )(*args)


CV_HALO = 16
CV_ROWS = 32


def _conv_kernel(prev_ref, main_ref, next_ref, dw_ref, dwb_ref, lng_ref, lnb_ref, o_ref, hs_scr):
    i = pl.program_id(1)
    n = pl.num_programs(1)
    tm = main_ref.shape[0]

    def glu(u):
        a = u[:, :CV_WIDTH].astype(F32)
        g = u[:, CV_WIDTH:].astype(F32)
        return a * _sigmoid(g)

    hs_scr[0:CV_HALO, :] = jnp.where(i > 0, glu(prev_ref[...]), 0.0)
    hs_scr[CV_HALO:CV_HALO + tm, :] = glu(main_ref[...])
    hs_scr[CV_HALO + tm:2 * CV_HALO + tm, :] = jnp.where(i < n - 1, glu(next_ref[...]), 0.0)

    off = CV_HALO - CV_KERNEL // 2
    for r in range(0, tm, CV_ROWS):
        acc = jnp.zeros((CV_ROWS, CV_WIDTH), F32) + dwb_ref[...]
        for j in range(CV_KERNEL):
            acc = acc + hs_scr[r + off + j:r + off + j + CV_ROWS, :] * dw_ref[j:j + 1, :]
        mu = jnp.mean(acc, axis=-1, keepdims=True)
        xc = acc - mu
        var = jnp.mean(xc * xc, axis=-1, keepdims=True)
        y = xc * lax.rsqrt(var + EPS) * lng_ref[...] + lnb_ref[...]
        o_ref[r:r + CV_ROWS, :] = (y * _sigmoid(y)).astype(BF16)


def _conv_call(z3, dw, dwb, lng, lnb, tm):
    b, l, _ = z3.shape
    nh = tm // CV_HALO
    last = l // CV_HALO - 1
    cw = 2 * CV_WIDTH
    vec = lambda v: v.reshape(1, CV_WIDTH)
    return pl.pallas_call(
        _conv_kernel,
        grid=(b, l // tm),
        in_specs=[pl.BlockSpec((None, CV_HALO, cw), lambda bi, i: (bi, jnp.maximum(i * nh - 1, 0), ZB_CIN)),
                  pl.BlockSpec((None, tm, cw), lambda bi, i: (bi, i, ZB_CIN)),
                  pl.BlockSpec((None, CV_HALO, cw), lambda bi, i: (bi, jnp.minimum((i + 1) * nh, last), ZB_CIN)),
                  pl.BlockSpec((CV_KERNEL, CV_WIDTH), lambda bi, i: (0, 0)),
                  pl.BlockSpec((1, CV_WIDTH), lambda bi, i: (0, 0)),
                  pl.BlockSpec((1, CV_WIDTH), lambda bi, i: (0, 0)),
                  pl.BlockSpec((1, CV_WIDTH), lambda bi, i: (0, 0))],
        out_specs=pl.BlockSpec((None, tm, CV_WIDTH), lambda bi, i: (bi, i, 0)),
        out_shape=jax.ShapeDtypeStruct((b, l, CV_WIDTH), BF16),
        scratch_shapes=[pltpu.VMEM((tm + 2 * CV_HALO, CV_WIDTH), F32)],
        compiler_params=_cparams(("arbitrary", "arbitrary")),
        name="conv",
    )(z3, z3, z3, dw, vec(dwb), vec(lng), vec(lnb))


def _gla_kernel(q_ref, k_ref, v_ref, gg_ref, ga_ref, aup_ref, ab_ref, gn_ref, s0_ref,
                y_ref, sfin_ref, la_scr, of_scr, ob_scr, st_scr):
    seq = q_ref.shape[0]
    n_chunks = seq // GLA_CHUNK
    ck = GLA_CHUNK
    rb = min(512, seq)

    for r in range(0, seq, rb):
        x = jnp.dot(ga_ref[r:r + rb, :], aup_ref[...], preferred_element_type=F32) + ab_ref[...]
        ls = jnp.minimum(x, 0.0) - jnp.log1p(jnp.exp(-jnp.abs(x)))
        la_scr[r:r + rb, :] = ls * (1.0 / GLA_NORMALIZER)

    st_scr[...] = s0_ref[...]

    row = lax.broadcasted_iota(jnp.int32, (ck, ck), 0)
    col = lax.broadcasted_iota(jnp.int32, (ck, ck), 1)
    tri = (row >= col, row <= col)
    tri_f = tuple(t.astype(F32) for t in tri)
    tri4 = tuple(jnp.concatenate([t] * GLA_HEADS, axis=0) for t in tri)
    hrow = lax.broadcasted_iota(jnp.int32, (GLA_HEADS * ck, GLA_KW), 0) // ck
    hlane = lax.broadcasted_iota(jnp.int32, (GLA_HEADS * ck, GLA_KW), 1) // GLA_DK
    head_mask = hrow == hlane

    def body(n, carry):
        for d in range(2):
            c = n if d == 0 else n_chunks - 1 - n
            r0 = pl.multiple_of(c * ck, ck)
            la = la_scr[pl.ds(r0, ck), d * GLA_KW:(d + 1) * GLA_KW]
            b = jnp.dot(tri_f[d], la, preferred_element_type=F32, precision=HIGHEST)
            b_last = b[ck - 1:ck, :] if d == 0 else b[0:1, :]
            q = q_ref[pl.ds(r0, ck), :].astype(F32) * (GLA_DK ** -0.5)
            k = k_ref[pl.ds(r0, ck), :].astype(F32)
            v = v_ref[pl.ds(r0, ck), :]
            q_dec = (q * jnp.exp(b)).astype(BF16)
            k_inv = (k * jnp.exp(-b)).astype(BF16)
            k_tail = k * jnp.exp(b_last - b)
            decay = jnp.exp(b_last)
            qm = jnp.where(head_mask, jnp.concatenate([q_dec] * GLA_HEADS, axis=0),
                           jnp.zeros((GLA_HEADS * ck, GLA_KW), BF16))
            sc = lax.dot_general(qm, k_inv, (((1,), (1,)), ((), ())), preferred_element_type=F32)
            sc = jnp.where(tri4[d], sc, 0.0).astype(BF16)
            state = st_scr[d]
            o_inter = jnp.dot(qm, state.astype(BF16), preferred_element_type=F32)
            xt = jnp.concatenate([k_tail, jnp.broadcast_to(decay, (ck, GLA_KW))], axis=0).T
            kt_t = xt[:, :ck].astype(BF16)
            dcol = xt[:, ck:ck + 1]
            o_parts, ds_parts = [], []
            for h in range(GLA_HEADS):
                vh = v[:, h * GLA_DV:(h + 1) * GLA_DV]
                o_parts.append(jnp.dot(sc[h * ck:(h + 1) * ck], vh, preferred_element_type=F32)
                               + o_inter[h * ck:(h + 1) * ck])
                ds_parts.append(jnp.dot(kt_t[h * ck:(h + 1) * ck], vh, preferred_element_type=F32))
            o = jnp.concatenate(o_parts, axis=1)
            if d == 0:
                of_scr[pl.ds(r0, ck), :] = o
            else:
                ob_scr[pl.ds(r0, ck), :] = o
            st_scr[d] = dcol * state + jnp.concatenate(ds_parts, axis=0)
        return carry

    lax.fori_loop(0, n_chunks, body, 0)
    sfin_ref[...] = st_scr[...]

    for r in range(0, seq, rb):
        o = of_scr[r:r + rb, :] + ob_scr[r:r + rb, :]
        gg = gg_ref[r:r + rb, :].astype(F32)
        for h in range(GLA_HEADS):
            sl = slice(h * GLA_DV, (h + 1) * GLA_DV)
            gh = gg[:, sl]
            y_ref[r:r + rb, sl] = (_rms(o[:, sl], gn_ref[...]) * (gh * _sigmoid(gh))).astype(BF16)


def _gla_call(z3, ga3, aup, ab, gnorm, s0):
    b, l, _ = z3.shape
    st_shape = (2, GLA_HEADS * GLA_DK, GLA_DV)
    return pl.pallas_call(
        _gla_kernel,
        grid=(b,),
        in_specs=[pl.BlockSpec((None, l, GLA_KW), lambda bi: (bi, 0, ZB_GQ)),
                  pl.BlockSpec((None, l, GLA_KW), lambda bi: (bi, 0, ZB_GK)),
                  pl.BlockSpec((None, l, GLA_VW), lambda bi: (bi, 0, ZB_GV)),
                  pl.BlockSpec((None, l, GLA_VW), lambda bi: (bi, 0, ZB_GG)),
                  pl.BlockSpec((None, l, LANES), lambda bi: (bi, 0, 0)),
                  pl.BlockSpec((LANES, 2 * GLA_KW), lambda bi: (0, 0)),
                  pl.BlockSpec((1, 2 * GLA_KW), lambda bi: (0, 0)),
                  pl.BlockSpec((1, GLA_DV), lambda bi: (0, 0)),
                  pl.BlockSpec((None,) + st_shape, lambda bi: (bi, 0, 0, 0))],
        out_specs=[pl.BlockSpec((None, l, GLA_VW), lambda bi: (bi, 0, 0)),
                   pl.BlockSpec((None,) + st_shape, lambda bi: (bi, 0, 0, 0))],
        out_shape=[jax.ShapeDtypeStruct((b, l, GLA_VW), BF16),
                   jax.ShapeDtypeStruct((b,) + st_shape, F32)],
        scratch_shapes=[pltpu.VMEM((l, 2 * GLA_KW), F32),
                        pltpu.VMEM((l, GLA_VW), F32),
                        pltpu.VMEM((l, GLA_VW), F32),
                        pltpu.VMEM(st_shape, F32)],
        compiler_params=_cparams(("arbitrary",)),
        name="gla",
    )(z3, z3, z3, z3, ga3, aup, ab, gnorm.reshape(1, GLA_DV), s0)


def _merge_kernel(ya_ref, yb_ref, yc_ref, g0_ref, g1_ref, g2_ref, x_ref, gate_ref, sh_ref, sc_ref,
                  gpost_ref, gpre_ref, wa_ref, wb_ref, wc_ref, wo_ref, xo_ref, h_ref):
    ya = jnp.dot(ya_ref[...], wa_ref[...], preferred_element_type=F32)
    yb = jnp.dot(yb_ref[...], wb_ref[...], preferred_element_type=F32)
    yc = jnp.dot(yc_ref[...], wc_ref[...], preferred_element_type=F32)
    mix = (_sigmoid(g0_ref[...].astype(F32)) * ya + _sigmoid(g1_ref[...].astype(F32)) * yb
           + _sigmoid(g2_ref[...].astype(F32)) * yc)
    y = jnp.dot(mix.astype(BF16), wo_ref[...], preferred_element_type=F32)
    xn = x_ref[...] + gate_ref[...] * _rms(y, gpost_ref[...])
    xo_ref[...] = xn
    h_ref[...] = (_rms(xn, gpre_ref[...]) * (1.0 + sc_ref[...]) + sh_ref[...]).astype(BF16)


def _merge_call(ya, yb, yc, z, x2, mod3, g_post, g_pre, wa, wb, wc, wo, rows_per_mod, mod_row0, tm):
    t, d = x2.shape
    modspec = lambda k: pl.BlockSpec((None, 1, d), lambda i: (mod_row0 + i // rows_per_mod, 0, k))
    full = lambda a: pl.BlockSpec(a.shape, lambda i: (0,) * a.ndim)
    row = lambda w: pl.BlockSpec((tm, w), lambda i: (i, 0))
    in_specs = [row(DA_WIDTH), row(CV_WIDTH), row(GLA_VW),
                pl.BlockSpec((tm, d), lambda i: (i, ZB_GATES)),
                pl.BlockSpec((tm, d), lambda i: (i, ZB_GATES + 1)),
                pl.BlockSpec((tm, d), lambda i: (i, ZB_GATES + 2)),
                row(d), modspec(2), modspec(3), modspec(4),
                pl.BlockSpec((1, d), lambda i: (0, 0)), pl.BlockSpec((1, d), lambda i: (0, 0)),
                full(wa), full(wb), full(wc), full(wo)]
    return pl.pallas_call(
        _merge_kernel,
        grid=(t // tm,),
        in_specs=in_specs,
        out_specs=[row(d), row(d)],
        out_shape=[jax.ShapeDtypeStruct((t, d), F32), jax.ShapeDtypeStruct((t, d), BF16)],
        compiler_params=_cparams(("arbitrary",)),
        name="merge",
    )(ya, yb, yc, z, z, z, x2, mod3, mod3, mod3, g_post.reshape(1, d), g_pre.reshape(1, d), wa, wb, wc, wo)


FF_CHUNK = MXU_COLS


def _swiglu(h, w1_ref, w3_ref, w2_ref):
    acc = jnp.zeros((h.shape[0], w2_ref.shape[1]), F32)
    for c in range(0, w1_ref.shape[1], FF_CHUNK):
        u = jnp.dot(h, w1_ref[:, c:c + FF_CHUNK], preferred_element_type=F32)
        g = jnp.dot(h, w3_ref[:, c:c + FF_CHUNK], preferred_element_type=F32)
        a = (u * _sigmoid(u) * g).astype(BF16)
        acc = acc + jnp.dot(a, w2_ref[c:c + FF_CHUNK, :], preferred_element_type=F32)
    return acc


def _ffn_kernel(h_ref, x_ref, gate_ref, gpost_ref, w1_ref, w3_ref, w2_ref, xo_ref):
    f = _swiglu(h_ref[...], w1_ref, w3_ref, w2_ref)
    xo_ref[...] = x_ref[...] + gate_ref[...] * _rms(f, gpost_ref[...])


def _ffn_call(h2, x2, mod3, g_post, w1, w3, w2, rows_per_mod, mod_row0, tm):
    t, d = x2.shape
    full = lambda a: pl.BlockSpec(a.shape, lambda i: (0,) * a.ndim)
    row = pl.BlockSpec((tm, d), lambda i: (i, 0))
    return pl.pallas_call(
        _ffn_kernel,
        grid=(t // tm,),
        in_specs=[row, row,
                  pl.BlockSpec((None, 1, d), lambda i: (mod_row0 + i // rows_per_mod, 0, 5)),
                  pl.BlockSpec((1, d), lambda i: (0, 0)), full(w1), full(w3), full(w2)],
        out_specs=row,
        out_shape=jax.ShapeDtypeStruct((t, d), F32),
        compiler_params=_cparams(("arbitrary",)),
        name="ffn",
    )(h2, x2, mod3, g_post.reshape(1, d), w1, w3, w2)


def _top2_gate(logits):
    lane = lax.broadcasted_iota(jnp.int32, logits.shape, 1)
    neg = jnp.float32(-jnp.inf)
    lg = jnp.where(lane < N_EXPERTS, logits, neg)
    m1 = lg.max(axis=-1, keepdims=True)
    i1 = jnp.where(lg == m1, lane, LANES).min(axis=-1, keepdims=True)
    lg2 = jnp.where(lane == i1, neg, lg)
    m2 = lg2.max(axis=-1, keepdims=True)
    i2 = jnp.where(lg2 == m2, lane, LANES).min(axis=-1, keepdims=True)
    e2 = jnp.exp(m2 - m1)
    w1 = 1.0 / (1.0 + e2)
    return jnp.where(lane == i1, w1, jnp.where(lane == i2, e2 * w1, 0.0))


def _moe_kernel(h_ref, x_ref, gate_ref, gpost_ref, rt_ref, w1_ref, w3_ref, w2_ref, xo_ref,
                acc_scr, gt_scr):
    e = pl.program_id(1)

    @pl.when(e == 0)
    def _():
        logits = jnp.dot(h_ref[...], rt_ref[...], preferred_element_type=F32)
        gt_scr[...] = _top2_gate(logits)
        acc_scr[...] = jnp.zeros_like(acc_scr)

    lane = lax.broadcasted_iota(jnp.int32, gt_scr.shape, 1)
    ge = jnp.sum(jnp.where(lane == e, gt_scr[...], 0.0), axis=-1, keepdims=True)
    acc_scr[...] += ge * _swiglu(h_ref[...], w1_ref, w3_ref, w2_ref)

    @pl.when(e == pl.num_programs(1) - 1)
    def _():
        xo_ref[...] = x_ref[...] + gate_ref[...] * _rms(acc_scr[...], gpost_ref[...])


def _moe_call(h2, x2, mod3, g_post, router_p, w1, w3, w2, rows_per_mod, tm):
    t, d = x2.shape
    ne, _, ff = w1.shape
    row = pl.BlockSpec((tm, d), lambda i, e: (i, 0))
    return pl.pallas_call(
        _moe_kernel,
        grid=(t // tm, ne),
        in_specs=[row, row,
                  pl.BlockSpec((None, 1, d), lambda i, e: (i // rows_per_mod, 0, 5)),
                  pl.BlockSpec((1, d), lambda i, e: (0, 0)),
                  pl.BlockSpec((d, LANES), lambda i, e: (0, 0)),
                  pl.BlockSpec((None, d, ff), lambda i, e: (e, 0, 0)),
                  pl.BlockSpec((None, d, ff), lambda i, e: (e, 0, 0)),
                  pl.BlockSpec((None, ff, d), lambda i, e: (e, 0, 0))],
        out_specs=row,
        out_shape=jax.ShapeDtypeStruct((t, d), F32),
        scratch_shapes=[pltpu.VMEM((tm, d), F32), pltpu.VMEM((tm, LANES), F32)],
        compiler_params=_cparams(("arbitrary", "arbitrary")),
        name="moe",
    )(h2, x2, mod3, g_post.reshape(1, d), router_p, w1, w3, w2)


def _prep_w_in(w):
    o_q, o_k, o_v, o_c, o_gq, o_gk, o_gv, o_gg, o_ga, o_gt = 0, 512, 1024, 1536, 2560, 2816, 3072, 3584, 4096, 4128
    main = jnp.concatenate([w[:, o_q:o_k], w[:, o_k:o_v], w[:, o_v:o_c], w[:, o_gv:o_gg], w[:, o_c:o_gq],
                            w[:, o_gg:o_ga], w[:, o_gq:o_gk], w[:, o_gk:o_gv], w[:, o_gt:]], axis=1)
    ga = jnp.pad(w[:, o_ga:o_gt], ((0, 0), (0, LANES - 2 * GLA_RANK)))
    return main.astype(BF16), ga.astype(BF16)


def _prep_a_up(a_up, a_b):
    m = jnp.zeros((LANES, 2 * GLA_KW), F32)
    m = m.at[:GLA_RANK, :GLA_KW].set(a_up[0]).at[GLA_RANK:2 * GLA_RANK, GLA_KW:].set(a_up[1])
    return m.astype(BF16), a_b.reshape(1, 2 * GLA_KW).astype(F32)


def _row_tile(n, pref):
    t = min(n, pref)
    assert n % t == 0
    return t


def kernel(x, c, ctx, c_ctx, ada_w, ada_b, g_mix_pre, g_mix_post, g_ffn_pre, g_ffn_post, w_in, da_lambda, da_subln, da_proj, cv_dw, cv_dw_b, cv_ln_g, cv_ln_b, cv_proj, gla_a_up, gla_a_b, gla_norm, gla_proj, w_out, ffn_w1, ffn_w3, ffn_w2, moe_router, moe_w1, moe_w3, moe_w2):
    bsz, seq, d = x.shape
    lc = ctx.shape[1]
    depth = w_in.shape[0]
    assert bsz + 1 <= MOD_ROWS and d == D_MODEL
    t_lat, t_ctx = bsz * seq, bsz * lc

    cc = jnp.zeros((MOD_ROWS, d), F32).at[:bsz].set(c).at[bsz].set(c_ctx)
    mod_all = _mod_call(cc, ada_w, ada_b)
    rope_tabs = _rope_tables(seq)

    tm_lat = _row_tile(seq, 1024)
    tm_ctx = _row_tile(t_ctx, 1024)
    tm_mix = _row_tile(seq, 512)
    tm_mix_ctx = _row_tile(t_ctx, 512)
    tq = _row_tile(seq, 128)
    tq_ctx = _row_tile(lc, 128)
    tc = _row_tile(seq, 256)
    tc_ctx = _row_tile(lc, 256)

    xl = x.reshape(t_lat, d)
    xc = ctx.reshape(t_ctx, d)
    for l in range(depth):
        need_ctx = l < depth - 1
        lambda_init = 0.8 - 0.6 * math.exp(-0.3 * l)
        mod3 = mod_all[l].reshape(MOD_ROWS, 1, 6 * d)
        w_main, w_ga = _prep_w_in(w_in[l])
        aup, ab = _prep_a_up(gla_a_up[l], gla_a_b[l])
        wa, wb, wc, wo = (da_proj[l].astype(BF16), cv_proj[l].astype(BF16),
                          gla_proj[l].astype(BF16), w_out[l].astype(BF16))

        z, ga = _inproj_call(xl, g_mix_pre[l], mod3, w_main, w_ga, rope_tabs, seq // tm_lat, 0, tm_lat)
        zc, gac = _inproj_call(xc, g_mix_pre[l], mod3, w_main, w_ga, None, t_ctx, bsz, tm_ctx)
        z3, zc3 = z.reshape(bsz, seq, Z_WIDTH), zc.reshape(bsz, lc, Z_WIDTH)

        ya = _attn_call(da_lambda[l], da_subln[l], z3, [z3, zc3], lambda_init, tq)
        yb = _conv_call(z3, cv_dw[l], cv_dw_b[l], cv_ln_g[l], cv_ln_b[l], tc)
        s_zero = jnp.zeros((bsz, 2, GLA_HEADS * GLA_DK, GLA_DV), F32)
        ycc, s_ctx = _gla_call(zc3, gac.reshape(bsz, lc, LANES), aup, ab, gla_norm[l], s_zero)
        yc, _ = _gla_call(z3, ga.reshape(bsz, seq, LANES), aup, ab, gla_norm[l], s_ctx)

        xl, h2 = _merge_call(ya.reshape(t_lat, -1), yb.reshape(t_lat, -1), yc.reshape(t_lat, -1), z, xl, mod3,
                             g_mix_post[l], g_ffn_pre[l], wa, wb, wc, wo, seq // tm_mix, 0, tm_mix)
        if need_ctx:
            yac = _attn_call(da_lambda[l], da_subln[l], zc3, [zc3], lambda_init, tq_ctx)
            ybc = _conv_call(zc3, cv_dw[l], cv_dw_b[l], cv_ln_g[l], cv_ln_b[l], tc_ctx)
            xc, hc2 = _merge_call(yac.reshape(t_ctx, -1), ybc.reshape(t_ctx, -1), ycc.reshape(t_ctx, -1), zc, xc,
                                  mod3, g_mix_post[l], g_ffn_pre[l], wa, wb, wc, wo, t_ctx, bsz, tm_mix_ctx)

        i = l // 2
        if l % 2 == 0:
            w1, w3, w2 = ffn_w1[i].astype(BF16), ffn_w3[i].astype(BF16), ffn_w2[i].astype(BF16)
            xl = _ffn_call(h2, xl, mod3, g_ffn_post[l], w1, w3, w2, seq // tm_mix, 0, tm_mix)
            if need_ctx:
                xc = _ffn_call(hc2, xc, mod3, g_ffn_post[l], w1, w3, w2, t_ctx, bsz, tm_mix_ctx)
        else:
            router_p = jnp.pad(moe_router[i], ((0, 0), (0, LANES - N_EXPERTS))).astype(BF16)
            w1, w3, w2 = moe_w1[i].astype(BF16), moe_w3[i].astype(BF16), moe_w2[i].astype(BF16)
            xl = _moe_call(h2, xl, mod3, g_ffn_post[l], router_p, w1, w3, w2, seq // tm_mix, tm_mix)
            if need_ctx:
                xc = _moe_call(hc2, xc, mod3, g_ffn_post[l], router_p, w1, w3, w2, t_ctx, tm_mix_ctx)
    return xl.reshape(bsz, seq, d)
```

```python
import functools
import math

import jax
import jax.numpy as jnp
from jax import lax
from jax.experimental import pallas as pl
from jax.experimental.pallas import tpu as pltpu

F32 = jnp.float32
BF16 = jnp.bfloat16
HIGHEST = lax.Precision.HIGHEST

D_MODEL = 1024
GRID_W = 64
DA_HEADS = 4
DA_HEAD_DIM = 64
DA_WIDTH = 512
ROPE_BASE = 10000.0
CV_WIDTH = 512
CV_KERNEL = 31
GLA_HEADS = 4
GLA_DK = 64
GLA_DV = 128
GLA_KW = 256
GLA_VW = 512
GLA_RANK = 16
GLA_NORMALIZER = 16.0
GLA_CHUNK = 64
N_EXPERTS = 8
D_FF = 2816
EPS = 1e-6
LOG2E = 1.4426950408889634

V7X_VMEM_BYTES = 64 * 1024 * 1024
VMEM_LIMIT = V7X_VMEM_BYTES - 8 * 1024 * 1024
LANES = 128
SUBLANES = 8
MXU_COLS = 256

Z_WIDTH = 7168
ZB_Q, ZB_K, ZB_V, ZB_GV = 0, 1, 2, 3
ZB_CIN = 2
ZB_GG = 6
ZB_GQ, ZB_GK = 14, 15
ZB_GATES = 4
MOD_ROWS = 40


def _cparams(sem, vmem=VMEM_LIMIT):
    return pltpu.CompilerParams(dimension_semantics=sem, vmem_limit_bytes=vmem)


def _sigmoid(x):
    return 1.0 / (1.0 + jnp.exp(-x))


def _rms(x, g):
    return x * lax.rsqrt(jnp.mean(x * x, axis=-1, keepdims=True) + EPS) * g


def _mod_kernel(c_ref, w_ref, b_ref, o_ref):
    c = c_ref[...]
    s = c * _sigmoid(c)
    o_ref[...] = jnp.dot(s, w_ref[...], preferred_element_type=F32, precision=HIGHEST) + b_ref[...]


def _mod_call(cc, ada_w, ada_b):
    depth, d, n6 = ada_w.shape
    tn = 1536
    return pl.pallas_call(
        _mod_kernel,
        grid=(depth, n6 // tn),
        in_specs=[pl.BlockSpec((MOD_ROWS, d), lambda l, j: (0, 0)),
                  pl.BlockSpec((None, d, tn), lambda l, j: (l, 0, j)),
                  pl.BlockSpec((None, 1, tn), lambda l, j: (l, 0, j))],
        out_specs=pl.BlockSpec((None, MOD_ROWS, tn), lambda l, j: (l, 0, j)),
        out_shape=jax.ShapeDtypeStruct((depth, MOD_ROWS, n6), F32),
        compiler_params=_cparams(("arbitrary", "arbitrary")),
        name="mod",
    )(cc, ada_w, ada_b.reshape(depth, 1, n6))


def _qk_lane_perm():
    n = jnp.arange(LANES)
    half, m = n // 64, n % 64
    return (m // 32) * 64 + ((m % 32) // 16) * 32 + half * 16 + m % 16


def _swap_halves(x):
    return pltpu.roll(x, LANES // 2, 1)


def _inproj_kernel(*refs, rope):
    if rope:
        (x_ref, g_ref, sh_ref, sc_ref, w_ref, wga_ref, cq_ref, sq_ref, ck_ref, sk_ref,
         z_ref, ga_ref, h_scr) = refs
    else:
        x_ref, g_ref, sh_ref, sc_ref, w_ref, wga_ref, z_ref, ga_ref, h_scr = refs
    j = pl.program_id(1)

    @pl.when(j == 0)
    def _():
        h = _rms(x_ref[...], g_ref[...]) * (1.0 + sc_ref[...]) + sh_ref[...]
        hb = h.astype(BF16)
        h_scr[...] = hb
        ga_ref[...] = jnp.dot(hb, wga_ref[...], preferred_element_type=F32).astype(BF16)

    acc = jnp.dot(h_scr[...], w_ref[...], preferred_element_type=F32)

    @pl.when(j == 0)
    def _():
        qscale = DA_HEAD_DIM ** -0.5 * LOG2E
        if rope:
            cols = []
            for c in range(acc.shape[1] // LANES):
                xs = acc[:, c * LANES:(c + 1) * LANES]
                is_q = c < DA_WIDTH // LANES
                cos = cq_ref[...] if is_q else ck_ref[...]
                sin = sq_ref[...] if is_q else sk_ref[...]
                cols.append((xs * cos + _swap_halves(xs) * sin).astype(BF16))
            z_ref[...] = jnp.concatenate(cols, axis=1)
        else:
            z_ref[:, :DA_WIDTH] = (acc[:, :DA_WIDTH] * qscale).astype(BF16)
            z_ref[:, DA_WIDTH:] = acc[:, DA_WIDTH:].astype(BF16)

    @pl.when(j != 0)
    def _():
        z_ref[...] = acc.astype(BF16)


def _inproj_call(x2, g, mod3, w_main, w_ga, rope_tabs, rows_per_mod, mod_row0, tm):
    t, d = x2.shape
    tn = 1024
    rope = rope_tabs is not None
    mrow = lambda i, j: (mod_row0 + i // rows_per_mod, 0, 0)
    in_specs = [pl.BlockSpec((tm, d), lambda i, j: (i, 0)),
                pl.BlockSpec((1, d), lambda i, j: (0, 0)),
                pl.BlockSpec((None, 1, d), mrow),
                pl.BlockSpec((None, 1, d), lambda i, j: (mod_row0 + i // rows_per_mod, 0, 1)),
                pl.BlockSpec((d, tn), lambda i, j: (0, j)),
                pl.BlockSpec((d, LANES), lambda i, j: (0, 0))]
    args = [x2, g.reshape(1, d), mod3, mod3, w_main, w_ga]
    if rope:
        nseq = rope_tabs[0].shape[0] // tm
        for tab in rope_tabs:
            in_specs.append(pl.BlockSpec((tm, LANES), lambda i, j: (i % nseq, 0)))
            args.append(tab)
    return pl.pallas_call(
        functools.partial(_inproj_kernel, rope=rope),
        grid=(t // tm, Z_WIDTH // tn),
        in_specs=in_specs,
        out_specs=[pl.BlockSpec((tm, tn), lambda i, j: (i, j)),
                   pl.BlockSpec((tm, LANES), lambda i, j: (i, 0))],
        out_shape=[jax.ShapeDtypeStruct((t, Z_WIDTH), BF16),
                   jax.ShapeDtypeStruct((t, LANES), BF16)],
        scratch_shapes=[pltpu.VMEM((tm, d), BF16)],
        compiler_params=_cparams(("arbitrary", "arbitrary")),
        name="inproj_rope" if rope else "inproj",
    )(*args)


def _rope_tables(seq):
    t = jnp.arange(seq)
    nf = DA_HEAD_DIM // 4
    inv = ROPE_BASE ** (-jnp.arange(nf, dtype=F32) / nf)
    ang_r = (t // GRID_W).astype(F32)[:, None] * inv
    ang_c = (t % GRID_W).astype(F32)[:, None] * inv
    cos32 = jnp.concatenate([jnp.cos(ang_r), jnp.cos(ang_c)], axis=1)
    sin32 = jnp.concatenate([jnp.sin(ang_r), jnp.sin(ang_c)], axis=1)
    cos = jnp.tile(cos32, (1, LANES // 32))
    sin = jnp.concatenate([-sin32, -sin32, sin32, sin32], axis=1)
    qscale = DA_HEAD_DIM ** -0.5 * LOG2E
    return cos * qscale, sin * qscale, cos, sin


ATT_ROWS = 16


def _attn_kernel(*refs, n_src, lambda_init):
    lam_ref, subln_ref, q_ref = refs[:3]
    kv_refs = refs[3:3 + 2 * n_src]
    o_ref = refs[3 + 2 * n_src]
    tq = q_ref.shape[0]
    hw = 2 * DA_HEAD_DIM

    lp = lam_ref[...]
    lam = (jnp.exp(jnp.sum(lp[0:1] * lp[1:2], axis=-1, keepdims=True))
           - jnp.exp(jnp.sum(lp[2:3] * lp[3:4], axis=-1, keepdims=True)) + lambda_init)
    lane = lax.broadcasted_iota(jnp.int32, (tq, hw), 1)
    first = (lane % 64) < 32

    for h in range(DA_HEADS):
        qh = q_ref[:, h * hw:(h + 1) * hw]
        zero = jnp.zeros_like(qh)
        qq = jnp.concatenate([jnp.where(first, qh, zero), jnp.where(first, zero, qh)], axis=0)
        scores = []
        for s in range(n_src):
            kh = kv_refs[2 * s][:, h * hw:(h + 1) * hw]
            scores.append(lax.dot_general(qq, kh, (((1,), (1,)), ((), ())),
                                          preferred_element_type=F32))
        probs = [[] for _ in range(n_src)]
        recips = []
        for rb in range(0, 2 * tq, ATT_ROWS):
            parts = [sc[rb:rb + ATT_ROWS, :] for sc in scores]
            m = parts[0].max(axis=-1, keepdims=True)
            for part in parts[1:]:
                m = jnp.maximum(m, part.max(axis=-1, keepdims=True))
            denom = None
            for s in range(n_src):
                p = jnp.exp2(parts[s] - m)
                ps = p.sum(axis=-1, keepdims=True)
                denom = ps if denom is None else denom + ps
                probs[s].append(p.astype(BF16))
            recips.append(1.0 / denom)
        r = jnp.concatenate(recips, axis=0)
        o0 = jnp.zeros((tq, hw), F32)
        o1 = jnp.zeros((tq, hw), F32)
        for s in range(n_src):
            pb = jnp.concatenate(probs[s], axis=0)
            vh = kv_refs[2 * s + 1][:, h * hw:(h + 1) * hw]
            o0 = o0 + jnp.dot(pb[:tq], vh, preferred_element_type=F32)
            o1 = o1 + jnp.dot(pb[tq:], vh, preferred_element_type=F32)
        o = o0 * r[:tq] - (lam * r[tq:]) * o1
        o = _rms(o, subln_ref[...]) * (1.0 - lambda_init)
        o_ref[:, h * hw:(h + 1) * hw] = o.astype(BF16)


def _attn_call(da_lambda, subln, zq3, kv_srcs, lambda_init, tq):
    b, lq, _ = zq3.shape
    in_specs = [pl.BlockSpec((4, DA_HEAD_DIM), lambda bi, i: (0, 0)),
                pl.BlockSpec((1, 2 * DA_HEAD_DIM), lambda bi, i: (0, 0)),
                pl.BlockSpec((None, tq, DA_WIDTH), lambda bi, i: (bi, i, ZB_Q))]
    args = [da_lambda, subln.reshape(1, -1), zq3]
    for src in kv_srcs:
        lk = src.shape[1]
        in_specs.append(pl.BlockSpec((None, lk, DA_WIDTH), lambda bi, i: (bi, 0, ZB_K)))
        in_specs.append(pl.BlockSpec((None, lk, DA_WIDTH), lambda bi, i: (bi, 0, ZB_V)))
        args += [src, src]
    return pl.pallas_call(
        functools.partial(_attn_kernel, n_src=len(kv_srcs), lambda_init=lambda_init),
        grid=(b, lq // tq),
        in_specs=in_specs,
        out_specs=pl.BlockSpec((None, tq, DA_WIDTH), lambda bi, i: (bi, i, 0)),
        out_shape=jax.ShapeDtypeStruct((b, lq, DA_WIDTH), BF16),
        compiler_params=_cparams(("arbitrary", "arbitrary")),
        name="attn",
    )(*args)


CV_HALO = 16
CV_ROWS = 32


def _conv_kernel(prev_ref, main_ref, next_ref, dw_ref, dwb_ref, lng_ref, lnb_ref, o_ref, hs_scr, sh_scr):
    i = pl.program_id(1)
    n = pl.num_programs(1)
    tm = main_ref.shape[0]

    def glu(u):
        a = u[:, :CV_WIDTH].astype(F32)
        g = u[:, CV_WIDTH:].astype(F32)
        return a * _sigmoid(g)

    hs_scr[0:CV_HALO, :] = jnp.where(i > 0, glu(prev_ref[...]), 0.0)
    hs_scr[CV_HALO:CV_HALO + tm, :] = glu(main_ref[...])
    hs_scr[CV_HALO + tm:2 * CV_HALO + tm, :] = jnp.where(i < n - 1, glu(next_ref[...]), 0.0)

    span = tm + 2 * CV_HALO - SUBLANES
    for s in range(SUBLANES):
        sh_scr[s, 0:span, :] = hs_scr[s:s + span, :]

    off = CV_HALO - CV_KERNEL // 2
    for r in range(0, tm, CV_ROWS):
        acc = jnp.zeros((CV_ROWS, CV_WIDTH), F32) + dwb_ref[...]
        for j in range(CV_KERNEL):
            s, a = (off + j) % SUBLANES, (off + j) // SUBLANES * SUBLANES
            acc = acc + sh_scr[s, r + a:r + a + CV_ROWS, :] * dw_ref[j:j + 1, :]
        mu = jnp.mean(acc, axis=-1, keepdims=True)
        xc = acc - mu
        var = jnp.mean(xc * xc, axis=-1, keepdims=True)
        y = xc * lax.rsqrt(var + EPS) * lng_ref[...] + lnb_ref[...]
        o_ref[r:r + CV_ROWS, :] = (y * _sigmoid(y)).astype(BF16)


def _conv_call(z3, dw, dwb, lng, lnb, tm):
    b, l, _ = z3.shape
    nh = tm // CV_HALO
    last = l // CV_HALO - 1
    cw = 2 * CV_WIDTH
    vec = lambda v: v.reshape(1, CV_WIDTH)
    return pl.pallas_call(
        _conv_kernel,
        grid=(b, l // tm),
        in_specs=[pl.BlockSpec((None, CV_HALO, cw), lambda bi, i: (bi, jnp.maximum(i * nh - 1, 0), ZB_CIN)),
                  pl.BlockSpec((None, tm, cw), lambda bi, i: (bi, i, ZB_CIN)),
                  pl.BlockSpec((None, CV_HALO, cw), lambda bi, i: (bi, jnp.minimum((i + 1) * nh, last), ZB_CIN)),
                  pl.BlockSpec((CV_KERNEL, CV_WIDTH), lambda bi, i: (0, 0)),
                  pl.BlockSpec((1, CV_WIDTH), lambda bi, i: (0, 0)),
                  pl.BlockSpec((1, CV_WIDTH), lambda bi, i: (0, 0)),
                  pl.BlockSpec((1, CV_WIDTH), lambda bi, i: (0, 0))],
        out_specs=pl.BlockSpec((None, tm, CV_WIDTH), lambda bi, i: (bi, i, 0)),
        out_shape=jax.ShapeDtypeStruct((b, l, CV_WIDTH), BF16),
        scratch_shapes=[pltpu.VMEM((tm + 2 * CV_HALO, CV_WIDTH), F32),
                        pltpu.VMEM((SUBLANES, tm + 2 * CV_HALO, CV_WIDTH), F32)],
        compiler_params=_cparams(("arbitrary", "arbitrary")),
        name="conv",
    )(z3, z3, z3, dw, vec(dwb), vec(lng), vec(lnb))


def _gla_kernel(q_ref, k_ref, v_ref, gg_ref, ga_ref, aup_ref, ab_ref, gn_ref, s0_ref,
                y_ref, sfin_ref, la_scr, of_scr, ob_scr, st_scr):
    seq = q_ref.shape[0]
    n_chunks = seq // GLA_CHUNK
    ck = GLA_CHUNK
    rb = min(512, seq)

    for r in range(0, seq, rb):
        x = jnp.dot(ga_ref[r:r + rb, :], aup_ref[...], preferred_element_type=F32) + ab_ref[...]
        ls = jnp.minimum(x, 0.0) - jnp.log1p(jnp.exp(-jnp.abs(x)))
        la_scr[r:r + rb, :] = ls * (1.0 / GLA_NORMALIZER)

    st_scr[...] = s0_ref[...]

    row = lax.broadcasted_iota(jnp.int32, (ck, ck), 0)
    col = lax.broadcasted_iota(jnp.int32, (ck, ck), 1)
    tri = (row >= col, row <= col)
    tri_f = tuple(t.astype(F32) for t in tri)
    tri4 = tuple(jnp.concatenate([t] * GLA_HEADS, axis=0) for t in tri)
    hrow = lax.broadcasted_iota(jnp.int32, (GLA_HEADS * ck, GLA_KW), 0) // ck
    hlane = lax.broadcasted_iota(jnp.int32, (GLA_HEADS * ck, GLA_KW), 1) // GLA_DK
    head_mask = hrow == hlane

    def body(n, carry):
        for d in range(2):
            c = n if d == 0 else n_chunks - 1 - n
            r0 = pl.multiple_of(c * ck, ck)
            la = la_scr[pl.ds(r0, ck), d * GLA_KW:(d + 1) * GLA_KW]
            b = jnp.dot(tri_f[d], la, preferred_element_type=F32, precision=HIGHEST)
            b_last = b[ck - 1:ck, :] if d == 0 else b[0:1, :]
            q = q_ref[pl.ds(r0, ck), :].astype(F32) * (GLA_DK ** -0.5)
            k = k_ref[pl.ds(r0, ck), :].astype(F32)
            v = v_ref[pl.ds(r0, ck), :]
            q_dec = (q * jnp.exp(b)).astype(BF16)
            k_inv = (k * jnp.exp(-b)).astype(BF16)
            k_tail = k * jnp.exp(b_last - b)
            decay = jnp.exp(b_last)
            qm = jnp.where(head_mask, jnp.concatenate([q_dec] * GLA_HEADS, axis=0),
                           jnp.zeros((GLA_HEADS * ck, GLA_KW), BF16))
            sc = lax.dot_general(qm, k_inv, (((1,), (1,)), ((), ())), preferred_element_type=F32)
            sc = jnp.where(tri4[d], sc, 0.0).astype(BF16)
            state = st_scr[d]
            o_inter = jnp.dot(qm, state.astype(BF16), preferred_element_type=F32)
            xt = jnp.concatenate([k_tail, jnp.broadcast_to(decay, (ck, GLA_KW))], axis=0).T
            kt_t = xt[:, :ck].astype(BF16)
            dcol = xt[:, ck:ck + 1]
            o_parts, ds_parts = [], []
            for h in range(GLA_HEADS):
                vh = v[:, h * GLA_DV:(h + 1) * GLA_DV]
                o_parts.append(jnp.dot(sc[h * ck:(h + 1) * ck], vh, preferred_element_type=F32)
                               + o_inter[h * ck:(h + 1) * ck])
                ds_parts.append(jnp.dot(kt_t[h * ck:(h + 1) * ck], vh, preferred_element_type=F32))
            o = jnp.concatenate(o_parts, axis=1)
            if d == 0:
                of_scr[pl.ds(r0, ck), :] = o
            else:
                ob_scr[pl.ds(r0, ck), :] = o
            st_scr[d] = dcol * state + jnp.concatenate(ds_parts, axis=0)
        return carry

    lax.fori_loop(0, n_chunks, body, 0, unroll=2)
    sfin_ref[...] = st_scr[...]

    for r in range(0, seq, rb):
        o = of_scr[r:r + rb, :] + ob_scr[r:r + rb, :]
        gg = gg_ref[r:r + rb, :].astype(F32)
        for h in range(GLA_HEADS):
            sl = slice(h * GLA_DV, (h + 1) * GLA_DV)
            gh = gg[:, sl]
            y_ref[r:r + rb, sl] = (_rms(o[:, sl], gn_ref[...]) * (gh * _sigmoid(gh))).astype(BF16)


def _gla_call(z3, ga3, aup, ab, gnorm, s0):
    b, l, _ = z3.shape
    st_shape = (2, GLA_HEADS * GLA_DK, GLA_DV)
    return pl.pallas_call(
        _gla_kernel,
        grid=(b,),
        in_specs=[pl.BlockSpec((None, l, GLA_KW), lambda bi: (bi, 0, ZB_GQ)),
                  pl.BlockSpec((None, l, GLA_KW), lambda bi: (bi, 0, ZB_GK)),
                  pl.BlockSpec((None, l, GLA_VW), lambda bi: (bi, 0, ZB_GV)),
                  pl.BlockSpec((None, l, GLA_VW), lambda bi: (bi, 0, ZB_GG)),
                  pl.BlockSpec((None, l, LANES), lambda bi: (bi, 0, 0)),
                  pl.BlockSpec((LANES, 2 * GLA_KW), lambda bi: (0, 0)),
                  pl.BlockSpec((1, 2 * GLA_KW), lambda bi: (0, 0)),
                  pl.BlockSpec((1, GLA_DV), lambda bi: (0, 0)),
                  pl.BlockSpec((None,) + st_shape, lambda bi: (bi, 0, 0, 0))],
        out_specs=[pl.BlockSpec((None, l, GLA_VW), lambda bi: (bi, 0, 0)),
                   pl.BlockSpec((None,) + st_shape, lambda bi: (bi, 0, 0, 0))],
        out_shape=[jax.ShapeDtypeStruct((b, l, GLA_VW), BF16),
                   jax.ShapeDtypeStruct((b,) + st_shape, F32)],
        scratch_shapes=[pltpu.VMEM((l, 2 * GLA_KW), F32),
                        pltpu.VMEM((l, GLA_VW), F32),
                        pltpu.VMEM((l, GLA_VW), F32),
                        pltpu.VMEM(st_shape, F32)],
        compiler_params=_cparams(("arbitrary",)),
        name="gla",
    )(z3, z3, z3, z3, ga3, aup, ab, gnorm.reshape(1, GLA_DV), s0)


def _merge_kernel(ya_ref, yb_ref, yc_ref, g0_ref, g1_ref, g2_ref, x_ref, gate_ref, sh_ref, sc_ref,
                  gpost_ref, gpre_ref, wa_ref, wb_ref, wc_ref, wo_ref, xo_ref, h_ref):
    ya = jnp.dot(ya_ref[...], wa_ref[...], preferred_element_type=F32)
    yb = jnp.dot(yb_ref[...], wb_ref[...], preferred_element_type=F32)
    yc = jnp.dot(yc_ref[...], wc_ref[...], preferred_element_type=F32)
    mix = (_sigmoid(g0_ref[...].astype(F32)) * ya + _sigmoid(g1_ref[...].astype(F32)) * yb
           + _sigmoid(g2_ref[...].astype(F32)) * yc)
    y = jnp.dot(mix.astype(BF16), wo_ref[...], preferred_element_type=F32)
    xn = x_ref[...] + gate_ref[...] * _rms(y, gpost_ref[...])
    xo_ref[...] = xn
    h_ref[...] = (_rms(xn, gpre_ref[...]) * (1.0 + sc_ref[...]) + sh_ref[...]).astype(BF16)


def _merge_call(ya, yb, yc, z, x2, mod3, g_post, g_pre, wa, wb, wc, wo, rows_per_mod, mod_row0, tm):
    t, d = x2.shape
    modspec = lambda k: pl.BlockSpec((None, 1, d), lambda i: (mod_row0 + i // rows_per_mod, 0, k))
    full = lambda a: pl.BlockSpec(a.shape, lambda i: (0,) * a.ndim)
    row = lambda w: pl.BlockSpec((tm, w), lambda i: (i, 0))
    in_specs = [row(DA_WIDTH), row(CV_WIDTH), row(GLA_VW),
                pl.BlockSpec((tm, d), lambda i: (i, ZB_GATES)),
                pl.BlockSpec((tm, d), lambda i: (i, ZB_GATES + 1)),
                pl.BlockSpec((tm, d), lambda i: (i, ZB_GATES + 2)),
                row(d), modspec(2), modspec(3), modspec(4),
                pl.BlockSpec((1, d), lambda i: (0, 0)), pl.BlockSpec((1, d), lambda i: (0, 0)),
                full(wa), full(wb), full(wc), full(wo)]
    return pl.pallas_call(
        _merge_kernel,
        grid=(t // tm,),
        in_specs=in_specs,
        out_specs=[row(d), row(d)],
        out_shape=[jax.ShapeDtypeStruct((t, d), F32), jax.ShapeDtypeStruct((t, d), BF16)],
        compiler_params=_cparams(("arbitrary",)),
        name="merge",
    )(ya, yb, yc, z, z, z, x2, mod3, mod3, mod3, g_post.reshape(1, d), g_pre.reshape(1, d), wa, wb, wc, wo)


FF_CHUNK = MXU_COLS


def _swiglu(h, w1_ref, w3_ref, w2_ref):
    acc = jnp.zeros((h.shape[0], w2_ref.shape[1]), F32)
    for c in range(0, w1_ref.shape[1], FF_CHUNK):
        u = jnp.dot(h, w1_ref[:, c:c + FF_CHUNK], preferred_element_type=F32)
        g = jnp.dot(h, w3_ref[:, c:c + FF_CHUNK], preferred_element_type=F32)
        a = (u * _sigmoid(u) * g).astype(BF16)
        acc = acc + jnp.dot(a, w2_ref[c:c + FF_CHUNK, :], preferred_element_type=F32)
    return acc


def _ffn_kernel(h_ref, x_ref, gate_ref, gpost_ref, w1_ref, w3_ref, w2_ref, xo_ref):
    f = _swiglu(h_ref[...], w1_ref, w3_ref, w2_ref)
    xo_ref[...] = x_ref[...] + gate_ref[...] * _rms(f, gpost_ref[...])


def _ffn_call(h2, x2, mod3, g_post, w1, w3, w2, rows_per_mod, mod_row0, tm):
    t, d = x2.shape
    full = lambda a: pl.BlockSpec(a.shape, lambda i: (0,) * a.ndim)
    row = pl.BlockSpec((tm, d), lambda i: (i, 0))
    return pl.pallas_call(
        _ffn_kernel,
        grid=(t // tm,),
        in_specs=[row, row,
                  pl.BlockSpec((None, 1, d), lambda i: (mod_row0 + i // rows_per_mod, 0, 5)),
                  pl.BlockSpec((1, d), lambda i: (0, 0)), full(w1), full(w3), full(w2)],
        out_specs=row,
        out_shape=jax.ShapeDtypeStruct((t, d), F32),
        compiler_params=_cparams(("arbitrary",)),
        name="ffn",
    )(h2, x2, mod3, g_post.reshape(1, d), w1, w3, w2)


def _top2_gate(logits):
    lane = lax.broadcasted_iota(jnp.int32, logits.shape, 1)
    neg = jnp.float32(-jnp.inf)
    lg = jnp.where(lane < N_EXPERTS, logits, neg)
    m1 = lg.max(axis=-1, keepdims=True)
    i1 = jnp.where(lg == m1, lane, LANES).min(axis=-1, keepdims=True)
    lg2 = jnp.where(lane == i1, neg, lg)
    m2 = lg2.max(axis=-1, keepdims=True)
    i2 = jnp.where(lg2 == m2, lane, LANES).min(axis=-1, keepdims=True)
    e2 = jnp.exp(m2 - m1)
    w1 = 1.0 / (1.0 + e2)
    gate = jnp.where(lane == i1, w1, jnp.where(lane == i2, e2 * w1, 0.0))
    sel = jnp.where(lane == i1, 1.0, jnp.where(lane == i2, 1.0, 0.0))
    return gate, sel


MOE_GROUP = 288


def _moe_kernel(h_ref, rt_ref, w1_ref, w3_ref, w2_ref, f_ref, gate_scr, rank_scr):
    e = pl.program_id(1)
    tb = h_ref.shape[0]

    @pl.when(e == 0)
    def _():
        logits = jnp.dot(h_ref[...], rt_ref[...], preferred_element_type=F32)
        gate, sel = _top2_gate(logits)
        gate_scr[...] = gate
        r = lax.broadcasted_iota(jnp.int32, (tb, tb), 0)
        c = lax.broadcasted_iota(jnp.int32, (tb, tb), 1)
        before = jnp.where(c < r, 1.0, 0.0).astype(BF16)
        rank = jnp.dot(before, sel.astype(BF16), preferred_element_type=F32)
        rank_scr[...] = jnp.where(sel > 0.5, rank, -1.0).T
        f_ref[...] = jnp.zeros_like(f_ref)

    lane = lax.broadcasted_iota(jnp.int32, gate_scr.shape, 1)
    gcol = jnp.sum(jnp.where(lane == e, gate_scr[...], 0.0), axis=-1, keepdims=True)
    rank_row = rank_scr[pl.ds(e, 1), :]
    count = jnp.sum(jnp.where(rank_row >= 0.0, 1.0, 0.0))

    for s in range(0, tb, MOE_GROUP):
        rows = min(MOE_GROUP, tb - s)

        @pl.when(count > s)
        def _():
            slot = lax.broadcasted_iota(jnp.int32, (rows, tb), 0).astype(F32) + float(s)
            onehot = jnp.where(rank_row == slot, 1.0, 0.0).astype(BF16)
            hs = jnp.dot(onehot, h_ref[...], preferred_element_type=F32).astype(BF16)
            y = _swiglu(hs, w1_ref, w3_ref, w2_ref).astype(BF16)
            back = lax.dot_general(onehot, y, (((0,), (0,)), ((), ())), preferred_element_type=F32)
            f_ref[...] += gcol * back


def _moe_call(h2, router_p, w1, w3, w2, tb):
    t, d = h2.shape
    ne, _, ff = w1.shape
    row = pl.BlockSpec((tb, d), lambda i, e: (i, 0))
    return pl.pallas_call(
        _moe_kernel,
        grid=(t // tb, ne),
        in_specs=[row,
                  pl.BlockSpec((d, LANES), lambda i, e: (0, 0)),
                  pl.BlockSpec((None, d, ff), lambda i, e: (e, 0, 0)),
                  pl.BlockSpec((None, d, ff), lambda i, e: (e, 0, 0)),
                  pl.BlockSpec((None, ff, d), lambda i, e: (e, 0, 0))],
        out_specs=row,
        out_shape=jax.ShapeDtypeStruct((t, d), F32),
        scratch_shapes=[pltpu.VMEM((tb, LANES), F32), pltpu.VMEM((LANES, tb), F32)],
        compiler_params=_cparams(("arbitrary", "arbitrary")),
        name="moe",
    )(h2, router_p, w1, w3, w2)


def _resid_kernel(f_ref, x_ref, gate_ref, gpost_ref, xo_ref):
    xo_ref[...] = x_ref[...] + gate_ref[...] * _rms(f_ref[...], gpost_ref[...])


def _resid_call(f, x2, mod3, g_post, rows_per_mod, mod_row0, tm):
    t, d = x2.shape
    row = pl.BlockSpec((tm, d), lambda i: (i, 0))
    return pl.pallas_call(
        _resid_kernel,
        grid=(t // tm,),
        in_specs=[row, row,
                  pl.BlockSpec((None, 1, d), lambda i: (mod_row0 + i // rows_per_mod, 0, 5)),
                  pl.BlockSpec((1, d), lambda i: (0, 0))],
        out_specs=row,
        out_shape=jax.ShapeDtypeStruct((t, d), F32),
        compiler_params=_cparams(("arbitrary",)),
        name="resid",
    )(f, x2, mod3, g_post.reshape(1, d))


def _prep_w_in(w):
    o_q, o_k, o_v, o_c, o_gq, o_gk, o_gv, o_gg, o_ga, o_gt = 0, 512, 1024, 1536, 2560, 2816, 3072, 3584, 4096, 4128
    perm = (jnp.arange(DA_WIDTH // LANES)[:, None] * LANES + _qk_lane_perm()[None, :]).reshape(-1)
    main = jnp.concatenate([w[:, o_q:o_k][:, perm], w[:, o_k:o_v][:, perm], w[:, o_v:o_c], w[:, o_gv:o_gg], w[:, o_c:o_gq],
                            w[:, o_gg:o_ga], w[:, o_gq:o_gk], w[:, o_gk:o_gv], w[:, o_gt:]], axis=1)
    ga = jnp.pad(w[:, o_ga:o_gt], ((0, 0), (0, LANES - 2 * GLA_RANK)))
    return main.astype(BF16), ga.astype(BF16)


def _prep_a_up(a_up, a_b):
    m = jnp.zeros((LANES, 2 * GLA_KW), F32)
    m = m.at[:GLA_RANK, :GLA_KW].set(a_up[0]).at[GLA_RANK:2 * GLA_RANK, GLA_KW:].set(a_up[1])
    return m.astype(BF16), a_b.reshape(1, 2 * GLA_KW).astype(F32)


def _row_tile(n, pref):
    t = min(n, pref)
    assert n % t == 0
    return t


def kernel(x, c, ctx, c_ctx, ada_w, ada_b, g_mix_pre, g_mix_post, g_ffn_pre, g_ffn_post, w_in, da_lambda, da_subln, da_proj, cv_dw, cv_dw_b, cv_ln_g, cv_ln_b, cv_proj, gla_a_up, gla_a_b, gla_norm, gla_proj, w_out, ffn_w1, ffn_w3, ffn_w2, moe_router, moe_w1, moe_w3, moe_w2):
    bsz, seq, d = x.shape
    lc = ctx.shape[1]
    depth = w_in.shape[0]
    assert bsz + 1 <= MOD_ROWS and d == D_MODEL
    t_lat, t_ctx = bsz * seq, bsz * lc

    cc = jnp.zeros((MOD_ROWS, d), F32).at[:bsz].set(c).at[bsz].set(c_ctx)
    mod_all = _mod_call(cc, ada_w, ada_b)
    rope_tabs = _rope_tables(seq)

    tm_lat = _row_tile(seq, 1024)
    tm_ctx = _row_tile(t_ctx, 1024)
    tm_mix = _row_tile(seq, 512)
    tm_mix_ctx = _row_tile(t_ctx, 512)
    tq = _row_tile(seq, 256)
    tq_ctx = _row_tile(lc, 128)
    tc = _row_tile(seq, 256)
    tc_ctx = _row_tile(lc, 256)

    xl = x.reshape(t_lat, d)
    xc = ctx.reshape(t_ctx, d)
    for l in range(depth):
        need_ctx = l < depth - 1
        lambda_init = 0.8 - 0.6 * math.exp(-0.3 * l)
        mod3 = mod_all[l].reshape(MOD_ROWS, 1, 6 * d)
        w_main, w_ga = _prep_w_in(w_in[l])
        aup, ab = _prep_a_up(gla_a_up[l], gla_a_b[l])
        wa, wb, wc, wo = (da_proj[l].astype(BF16), cv_proj[l].astype(BF16),
                          gla_proj[l].astype(BF16), w_out[l].astype(BF16))

        z, ga = _inproj_call(xl, g_mix_pre[l], mod3, w_main, w_ga, rope_tabs, seq // tm_lat, 0, tm_lat)
        zc, gac = _inproj_call(xc, g_mix_pre[l], mod3, w_main, w_ga, None, t_ctx, bsz, tm_ctx)
        z3, zc3 = z.reshape(bsz, seq, Z_WIDTH), zc.reshape(bsz, lc, Z_WIDTH)

        ya = _attn_call(da_lambda[l], da_subln[l], z3, [z3, zc3], lambda_init, tq)
        yb = _conv_call(z3, cv_dw[l], cv_dw_b[l], cv_ln_g[l], cv_ln_b[l], tc)
        s_zero = jnp.zeros((bsz, 2, GLA_HEADS * GLA_DK, GLA_DV), F32)
        ycc, s_ctx = _gla_call(zc3, gac.reshape(bsz, lc, LANES), aup, ab, gla_norm[l], s_zero)
        yc, _ = _gla_call(z3, ga.reshape(bsz, seq, LANES), aup, ab, gla_norm[l], s_ctx)

        xl, h2 = _merge_call(ya.reshape(t_lat, -1), yb.reshape(t_lat, -1), yc.reshape(t_lat, -1), z, xl, mod3,
                             g_mix_post[l], g_ffn_pre[l], wa, wb, wc, wo, seq // tm_mix, 0, tm_mix)
        if need_ctx:
            yac = _attn_call(da_lambda[l], da_subln[l], zc3, [zc3], lambda_init, tq_ctx)
            ybc = _conv_call(zc3, cv_dw[l], cv_dw_b[l], cv_ln_g[l], cv_ln_b[l], tc_ctx)
            xc, hc2 = _merge_call(yac.reshape(t_ctx, -1), ybc.reshape(t_ctx, -1), ycc.reshape(t_ctx, -1), zc, xc,
                                  mod3, g_mix_post[l], g_ffn_pre[l], wa, wb, wc, wo, t_ctx, bsz, tm_mix_ctx)

        i = l // 2
        if l % 2 == 0:
            w1, w3, w2 = ffn_w1[i].astype(BF16), ffn_w3[i].astype(BF16), ffn_w2[i].astype(BF16)
            xl = _ffn_call(h2, xl, mod3, g_ffn_post[l], w1, w3, w2, seq // tm_mix, 0, tm_mix)
            if need_ctx:
                xc = _ffn_call(hc2, xc, mod3, g_ffn_post[l], w1, w3, w2, t_ctx, bsz, tm_mix_ctx)
        else:
            router_p = jnp.pad(moe_router[i], ((0, 0), (0, LANES - N_EXPERTS))).astype(BF16)
            w1, w3, w2 = moe_w1[i].astype(BF16), moe_w3[i].astype(BF16), moe_w2[i].astype(BF16)
            f = _moe_call(h2, router_p, w1, w3, w2, _row_tile(t_lat, 1024))
            xl = _resid_call(f, xl, mod3, g_ffn_post[l], seq // tm_mix, 0, tm_mix)
            if need_ctx:
                fc = _moe_call(hc2, router_p, w1, w3, w2, _row_tile(t_ctx, 1024))
                xc = _resid_call(fc, xc, mod3, g_ffn_post[l], t_ctx, bsz, tm_mix_ctx)
    return xl.reshape(bsz, seq, d)
```

```python
import functools
import math

import jax
import jax.numpy as jnp
from jax import lax
from jax.experimental import pallas as pl
from jax.experimental.pallas import tpu as pltpu

F32 = jnp.float32
BF16 = jnp.bfloat16
HIGHEST = lax.Precision.HIGHEST

D_MODEL = 1024
GRID_W = 64
DA_HEADS = 4
DA_HEAD_DIM = 64
DA_WIDTH = 512
ROPE_BASE = 10000.0
CV_WIDTH = 512
CV_KERNEL = 31
GLA_HEADS = 4
GLA_DK = 64
GLA_DV = 128
GLA_KW = 256
GLA_VW = 512
GLA_RANK = 16
GLA_NORMALIZER = 16.0
GLA_CHUNK = 64
GLA_GROUP = 4
N_EXPERTS = 8
D_FF = 2816
EPS = 1e-6
LOG2E = 1.4426950408889634

V7X_VMEM_BYTES = 64 * 1024 * 1024
VMEM_LIMIT = V7X_VMEM_BYTES - 8 * 1024 * 1024
LANES = 128
SUBLANES = 8
MXU_COLS = 256

Z_WIDTH = 7168
ZB_Q, ZB_K, ZB_V, ZB_GV = 0, 1, 2, 3
ZB_CIN = 2
ZB_GG = 6
ZB_GQ, ZB_GK = 14, 15
ZB_GATES = 4
MOD_ROWS = 40


def _cparams(sem, vmem=VMEM_LIMIT):
    return pltpu.CompilerParams(dimension_semantics=sem, vmem_limit_bytes=vmem)


def _sigmoid(x):
    return 1.0 / (1.0 + jnp.exp(-x))


def _rms(x, g):
    return x * lax.rsqrt(jnp.mean(x * x, axis=-1, keepdims=True) + EPS) * g


def _mod_kernel(c_ref, w_ref, b_ref, o_ref):
    c = c_ref[...]
    s = c * _sigmoid(c)
    o_ref[...] = jnp.dot(s, w_ref[...], preferred_element_type=F32, precision=HIGHEST) + b_ref[...]


def _mod_call(cc, ada_w, ada_b):
    depth, d, n6 = ada_w.shape
    tn = 1536
    return pl.pallas_call(
        _mod_kernel,
        grid=(depth, n6 // tn),
        in_specs=[pl.BlockSpec((MOD_ROWS, d), lambda l, j: (0, 0)),
                  pl.BlockSpec((None, d, tn), lambda l, j: (l, 0, j)),
                  pl.BlockSpec((None, 1, tn), lambda l, j: (l, 0, j))],
        out_specs=pl.BlockSpec((None, MOD_ROWS, tn), lambda l, j: (l, 0, j)),
        out_shape=jax.ShapeDtypeStruct((depth, MOD_ROWS, n6), F32),
        compiler_params=_cparams(("arbitrary", "arbitrary")),
        name="mod",
    )(cc, ada_w, ada_b.reshape(depth, 1, n6))


def _qk_lane_perm():
    n = jnp.arange(LANES)
    half, m = n // 64, n % 64
    return (m // 32) * 64 + ((m % 32) // 16) * 32 + half * 16 + m % 16


def _swap_halves(x):
    return pltpu.roll(x, LANES // 2, 1)


def _inproj_kernel(*refs, rope):
    if rope:
        (x_ref, g_ref, sh_ref, sc_ref, w_ref, wga_ref, cq_ref, sq_ref, ck_ref, sk_ref,
         z_ref, ga_ref, h_scr) = refs
    else:
        x_ref, g_ref, sh_ref, sc_ref, w_ref, wga_ref, z_ref, ga_ref, h_scr = refs
    j = pl.program_id(1)

    @pl.when(j == 0)
    def _():
        h = _rms(x_ref[...], g_ref[...]) * (1.0 + sc_ref[...]) + sh_ref[...]
        hb = h.astype(BF16)
        h_scr[...] = hb
        ga_ref[...] = jnp.dot(hb, wga_ref[...], preferred_element_type=F32).astype(BF16)

    acc = jnp.dot(h_scr[...], w_ref[...], preferred_element_type=F32)

    @pl.when(j == 0)
    def _():
        qscale = DA_HEAD_DIM ** -0.5 * LOG2E
        if rope:
            cols = []
            for c in range(acc.shape[1] // LANES):
                xs = acc[:, c * LANES:(c + 1) * LANES]
                is_q = c < DA_WIDTH // LANES
                cos = cq_ref[...] if is_q else ck_ref[...]
                sin = sq_ref[...] if is_q else sk_ref[...]
                cols.append((xs * cos + _swap_halves(xs) * sin).astype(BF16))
            z_ref[...] = jnp.concatenate(cols, axis=1)
        else:
            z_ref[:, :DA_WIDTH] = (acc[:, :DA_WIDTH] * qscale).astype(BF16)
            z_ref[:, DA_WIDTH:] = acc[:, DA_WIDTH:].astype(BF16)

    @pl.when(j != 0)
    def _():
        z_ref[...] = acc.astype(BF16)


def _inproj_call(x2, g, mod3, w_main, w_ga, rope_tabs, rows_per_mod, mod_row0, tm):
    t, d = x2.shape
    tn = 1024
    rope = rope_tabs is not None
    mrow = lambda i, j: (mod_row0 + i // rows_per_mod, 0, 0)
    in_specs = [pl.BlockSpec((tm, d), lambda i, j: (i, 0)),
                pl.BlockSpec((1, d), lambda i, j: (0, 0)),
                pl.BlockSpec((None, 1, d), mrow),
                pl.BlockSpec((None, 1, d), lambda i, j: (mod_row0 + i // rows_per_mod, 0, 1)),
                pl.BlockSpec((d, tn), lambda i, j: (0, j)),
                pl.BlockSpec((d, LANES), lambda i, j: (0, 0))]
    args = [x2, g.reshape(1, d), mod3, mod3, w_main, w_ga]
    if rope:
        nseq = rope_tabs[0].shape[0] // tm
        for tab in rope_tabs:
            in_specs.append(pl.BlockSpec((tm, LANES), lambda i, j: (i % nseq, 0)))
            args.append(tab)
    return pl.pallas_call(
        functools.partial(_inproj_kernel, rope=rope),
        grid=(t // tm, Z_WIDTH // tn),
        in_specs=in_specs,
        out_specs=[pl.BlockSpec((tm, tn), lambda i, j: (i, j)),
                   pl.BlockSpec((tm, LANES), lambda i, j: (i, 0))],
        out_shape=[jax.ShapeDtypeStruct((t, Z_WIDTH), BF16),
                   jax.ShapeDtypeStruct((t, LANES), BF16)],
        scratch_shapes=[pltpu.VMEM((tm, d), BF16)],
        compiler_params=_cparams(("arbitrary", "arbitrary")),
        name="inproj_rope" if rope else "inproj",
    )(*args)


def _rope_tables(seq):
    t = jnp.arange(seq)
    nf = DA_HEAD_DIM // 4
    inv = ROPE_BASE ** (-jnp.arange(nf, dtype=F32) / nf)
    ang_r = (t // GRID_W).astype(F32)[:, None] * inv
    ang_c = (t % GRID_W).astype(F32)[:, None] * inv
    cos32 = jnp.concatenate([jnp.cos(ang_r), jnp.cos(ang_c)], axis=1)
    sin32 = jnp.concatenate([jnp.sin(ang_r), jnp.sin(ang_c)], axis=1)
    cos = jnp.tile(cos32, (1, LANES // 32))
    sin = jnp.concatenate([-sin32, -sin32, sin32, sin32], axis=1)
    qscale = DA_HEAD_DIM ** -0.5 * LOG2E
    return cos * qscale, sin * qscale, cos, sin


ATT_ROWS = 16


def _attn_kernel(*refs, n_src, lambda_init):
    lam_ref, subln_ref, q_ref = refs[:3]
    kv_refs = refs[3:3 + 2 * n_src]
    o_ref = refs[3 + 2 * n_src]
    tq = q_ref.shape[0]
    hw = 2 * DA_HEAD_DIM

    lp = lam_ref[...]
    lam = (jnp.exp(jnp.sum(lp[0:1] * lp[1:2], axis=-1, keepdims=True))
           - jnp.exp(jnp.sum(lp[2:3] * lp[3:4], axis=-1, keepdims=True)) + lambda_init)
    lane = lax.broadcasted_iota(jnp.int32, (tq, hw), 1)
    first = (lane % 64) < 32

    for h in range(DA_HEADS):
        qh = q_ref[:, h * hw:(h + 1) * hw]
        zero = jnp.zeros_like(qh)
        qq = jnp.concatenate([jnp.where(first, qh, zero), jnp.where(first, zero, qh)], axis=0)
        scores = []
        for s in range(n_src):
            kh = kv_refs[2 * s][:, h * hw:(h + 1) * hw]
            scores.append(lax.dot_general(qq, kh, (((1,), (1,)), ((), ())),
                                          preferred_element_type=F32))
        probs = [[] for _ in range(n_src)]
        recips = []
        for rb in range(0, 2 * tq, ATT_ROWS):
            parts = [sc[rb:rb + ATT_ROWS, :] for sc in scores]
            m = parts[0].max(axis=-1, keepdims=True)
            for part in parts[1:]:
                m = jnp.maximum(m, part.max(axis=-1, keepdims=True))
            denom = None
            for s in range(n_src):
                p = jnp.exp2(parts[s] - m)
                ps = p.sum(axis=-1, keepdims=True)
                denom = ps if denom is None else denom + ps
                probs[s].append(p.astype(BF16))
            recips.append(1.0 / denom)
        r = jnp.concatenate(recips, axis=0)
        o0 = jnp.zeros((tq, hw), F32)
        o1 = jnp.zeros((tq, hw), F32)
        for s in range(n_src):
            pb = jnp.concatenate(probs[s], axis=0)
            vh = kv_refs[2 * s + 1][:, h * hw:(h + 1) * hw]
            o0 = o0 + jnp.dot(pb[:tq], vh, preferred_element_type=F32)
            o1 = o1 + jnp.dot(pb[tq:], vh, preferred_element_type=F32)
        o = o0 * r[:tq] - (lam * r[tq:]) * o1
        o = _rms(o, subln_ref[...]) * (1.0 - lambda_init)
        o_ref[:, h * hw:(h + 1) * hw] = o.astype(BF16)


def _attn_call(da_lambda, subln, zq3, kv_srcs, lambda_init, tq):
    b, lq, _ = zq3.shape
    in_specs = [pl.BlockSpec((4, DA_HEAD_DIM), lambda bi, i: (0, 0)),
                pl.BlockSpec((1, 2 * DA_HEAD_DIM), lambda bi, i: (0, 0)),
                pl.BlockSpec((None, tq, DA_WIDTH), lambda bi, i: (bi, i, ZB_Q))]
    args = [da_lambda, subln.reshape(1, -1), zq3]
    for src in kv_srcs:
        lk = src.shape[1]
        in_specs.append(pl.BlockSpec((None, lk, DA_WIDTH), lambda bi, i: (bi, 0, ZB_K)))
        in_specs.append(pl.BlockSpec((None, lk, DA_WIDTH), lambda bi, i: (bi, 0, ZB_V)))
        args += [src, src]
    return pl.pallas_call(
        functools.partial(_attn_kernel, n_src=len(kv_srcs), lambda_init=lambda_init),
        grid=(b, lq // tq),
        in_specs=in_specs,
        out_specs=pl.BlockSpec((None, tq, DA_WIDTH), lambda bi, i: (bi, i, 0)),
        out_shape=jax.ShapeDtypeStruct((b, lq, DA_WIDTH), BF16),
        compiler_params=_cparams(("arbitrary", "arbitrary")),
        name="attn",
    )(*args)


CV_HALO = 16
CV_ROWS = 32


def _conv_kernel(prev_ref, main_ref, next_ref, dw_ref, dwb_ref, lng_ref, lnb_ref, o_ref, hs_scr, sh_scr):
    i = pl.program_id(1)
    n = pl.num_programs(1)
    tm = main_ref.shape[0]

    def glu(u):
        a = u[:, :CV_WIDTH].astype(F32)
        g = u[:, CV_WIDTH:].astype(F32)
        return a * _sigmoid(g)

    hs_scr[0:CV_HALO, :] = jnp.where(i > 0, glu(prev_ref[...]), 0.0)
    hs_scr[CV_HALO:CV_HALO + tm, :] = glu(main_ref[...])
    hs_scr[CV_HALO + tm:2 * CV_HALO + tm, :] = jnp.where(i < n - 1, glu(next_ref[...]), 0.0)

    span = tm + 2 * CV_HALO - SUBLANES
    for s in range(SUBLANES):
        sh_scr[s, 0:span, :] = hs_scr[s:s + span, :]

    off = CV_HALO - CV_KERNEL // 2
    for r in range(0, tm, CV_ROWS):
        acc = jnp.zeros((CV_ROWS, CV_WIDTH), F32) + dwb_ref[...]
        for j in range(CV_KERNEL):
            s, a = (off + j) % SUBLANES, (off + j) // SUBLANES * SUBLANES
            acc = acc + sh_scr[s, r + a:r + a + CV_ROWS, :] * dw_ref[j:j + 1, :]
        mu = jnp.mean(acc, axis=-1, keepdims=True)
        xc = acc - mu
        var = jnp.mean(xc * xc, axis=-1, keepdims=True)
        y = xc * lax.rsqrt(var + EPS) * lng_ref[...] + lnb_ref[...]
        o_ref[r:r + CV_ROWS, :] = (y * _sigmoid(y)).astype(BF16)


def _conv_call(z3, dw, dwb, lng, lnb, tm):
    b, l, _ = z3.shape
    nh = tm // CV_HALO
    last = l // CV_HALO - 1
    cw = 2 * CV_WIDTH
    vec = lambda v: v.reshape(1, CV_WIDTH)
    return pl.pallas_call(
        _conv_kernel,
        grid=(b, l // tm),
        in_specs=[pl.BlockSpec((None, CV_HALO, cw), lambda bi, i: (bi, jnp.maximum(i * nh - 1, 0), ZB_CIN)),
                  pl.BlockSpec((None, tm, cw), lambda bi, i: (bi, i, ZB_CIN)),
                  pl.BlockSpec((None, CV_HALO, cw), lambda bi, i: (bi, jnp.minimum((i + 1) * nh, last), ZB_CIN)),
                  pl.BlockSpec((CV_KERNEL, CV_WIDTH), lambda bi, i: (0, 0)),
                  pl.BlockSpec((1, CV_WIDTH), lambda bi, i: (0, 0)),
                  pl.BlockSpec((1, CV_WIDTH), lambda bi, i: (0, 0)),
                  pl.BlockSpec((1, CV_WIDTH), lambda bi, i: (0, 0))],
        out_specs=pl.BlockSpec((None, tm, CV_WIDTH), lambda bi, i: (bi, i, 0)),
        out_shape=jax.ShapeDtypeStruct((b, l, CV_WIDTH), BF16),
        scratch_shapes=[pltpu.VMEM((tm + 2 * CV_HALO, CV_WIDTH), F32),
                        pltpu.VMEM((SUBLANES, tm + 2 * CV_HALO, CV_WIDTH), F32)],
        compiler_params=_cparams(("arbitrary", "arbitrary")),
        name="conv",
    )(z3, z3, z3, dw, vec(dwb), vec(lng), vec(lnb))


def _exact_dot01(t01, x):
    hi = x.astype(BF16)
    r1 = x - hi.astype(F32)
    mid = r1.astype(BF16)
    lo = (r1 - mid.astype(F32)).astype(BF16)
    dot = lambda v: jnp.dot(t01, v, preferred_element_type=F32)
    return dot(hi) + dot(mid) + dot(lo)


def _gla_kernel(q_ref, k_ref, v_ref, gg_ref, ga_ref, aup_ref, ab_ref, gn_ref, s0_ref,
                y_ref, sfin_ref, o_scr, qd_scr, ds_scr, dt_scr, sin_scr, st_scr):
    seq = q_ref.shape[0]
    ck = GLA_CHUNK
    n_chunks = seq // ck
    cg = min(GLA_GROUP, n_chunks)
    rows = cg * ck
    n_groups = n_chunks // cg
    nh = GLA_HEADS

    ri = lax.broadcasted_iota(jnp.int32, (rows, rows), 0)
    ci = lax.broadcasted_iota(jnp.int32, (rows, rows), 1)
    same_chunk = (ri // ck) == (ci // ck)
    keep = (same_chunk & (ri >= ci), same_chunk & (ri <= ci))
    keep_bf = tuple(jnp.where(m, 1.0, 0.0).astype(BF16) for m in keep)
    lane_head = lax.broadcasted_iota(jnp.int32, (rows, GLA_KW), 1) // GLA_DK
    hrow = lax.broadcasted_iota(jnp.int32, (nh * ck, GLA_KW), 0) // ck
    hlane = lax.broadcasted_iota(jnp.int32, (nh * ck, GLA_KW), 1) // GLA_DK
    head_mask = hrow == hlane

    def phase_a(g, carry):
        r0 = pl.multiple_of(g * rows, rows)
        x = jnp.dot(ga_ref[pl.ds(r0, rows), :], aup_ref[...], preferred_element_type=F32) + ab_ref[...]
        la2 = (jnp.minimum(x, 0.0) - jnp.log(1.0 + jnp.exp(-jnp.abs(x)))) * (1.0 / GLA_NORMALIZER)
        q = q_ref[pl.ds(r0, rows), :].astype(F32) * (GLA_DK ** -0.5)
        k = k_ref[pl.ds(r0, rows), :].astype(F32)
        v = v_ref[pl.ds(r0, rows), :]
        o_parts = [jnp.zeros((rows, GLA_DV), F32) for _ in range(nh)]
        for d in range(2):
            la = la2[:, d * GLA_KW:(d + 1) * GLA_KW]
            b = _exact_dot01(keep_bf[d], la)
            last = ck - 1 if d == 0 else 0
            b_last = jnp.concatenate(
                [jnp.broadcast_to(b[c * ck + last:c * ck + last + 1, :], (ck, GLA_KW)) for c in range(cg)], axis=0)
            q_dec = (q * jnp.exp(b)).astype(BF16)
            k_inv = (k * jnp.exp(-b)).astype(BF16)
            kt_t = (k * jnp.exp(b_last - b)).T.astype(BF16)
            qd_scr[d, pl.ds(r0, rows), :] = q_dec
            dt_scr[d, g] = jnp.exp(b_last).T
            for h in range(nh):
                qm = jnp.where(lane_head == h, q_dec, jnp.zeros_like(q_dec))
                sc = lax.dot_general(qm, k_inv, (((1,), (1,)), ((), ())), preferred_element_type=F32)
                sc = jnp.where(keep[d], sc, 0.0).astype(BF16)
                vh = v[:, h * GLA_DV:(h + 1) * GLA_DV]
                o_parts[h] = o_parts[h] + jnp.dot(sc, vh, preferred_element_type=F32)
                kth = jnp.concatenate([kt_t[h * GLA_DK:(h + 1) * GLA_DK, :]] * cg, axis=0)
                lhs = jnp.where(same_chunk, kth, jnp.zeros_like(kth))
                ds = jnp.dot(lhs, vh, preferred_element_type=F32)
                for c in range(cg):
                    ds_scr[d, g * cg + c, h * GLA_DK:(h + 1) * GLA_DK, :] = ds[c * ck:(c + 1) * ck, :]
        o_scr[pl.ds(r0, rows), :] = jnp.concatenate(o_parts, axis=1)
        return carry

    lax.fori_loop(0, n_groups, phase_a, 0, unroll=min(2, n_groups))

    st_scr[...] = s0_ref[...]

    def phase_b(i, carry):
        for d in range(2):
            g = i if d == 0 else n_groups - 1 - i
            dt = dt_scr[d, g]
            for j in range(cg):
                c = j if d == 0 else cg - 1 - j
                state = st_scr[d]
                sin_scr[d, g * cg + c] = state.astype(BF16)
                st_scr[d] = dt[:, c * ck:c * ck + 1] * state + ds_scr[d, g * cg + c]
        return carry

    lax.fori_loop(0, n_groups, phase_b, 0)
    sfin_ref[...] = st_scr[...]

    def phase_c(g, carry):
        r0 = pl.multiple_of(g * rows, rows)
        o = o_scr[pl.ds(r0, rows), :]
        inter = []
        for c in range(cg):
            rc = pl.multiple_of(r0 + c * ck, ck)
            acc = jnp.zeros((nh * ck, GLA_DV), F32)
            for d in range(2):
                qd = qd_scr[d, pl.ds(rc, ck), :]
                qm = jnp.where(head_mask, jnp.concatenate([qd] * nh, axis=0), jnp.zeros((nh * ck, GLA_KW), BF16))
                acc = acc + jnp.dot(qm, sin_scr[d, g * cg + c], preferred_element_type=F32)
            inter.append(jnp.concatenate([acc[h * ck:(h + 1) * ck, :] for h in range(nh)], axis=1))
        o = o + jnp.concatenate(inter, axis=0)
        gg = gg_ref[pl.ds(r0, rows), :].astype(F32)
        for h in range(nh):
            sl = slice(h * GLA_DV, (h + 1) * GLA_DV)
            gh = gg[:, sl]
            y_ref[pl.ds(r0, rows), sl] = (_rms(o[:, sl], gn_ref[...]) * (gh * _sigmoid(gh))).astype(BF16)
        return carry

    lax.fori_loop(0, n_groups, phase_c, 0, unroll=min(2, n_groups))


def _gla_call(z3, ga3, aup, ab, gnorm, s0):
    b, l, _ = z3.shape
    st_shape = (2, GLA_HEADS * GLA_DK, GLA_DV)
    n_chunks = l // GLA_CHUNK
    cg = min(GLA_GROUP, n_chunks)
    return pl.pallas_call(
        _gla_kernel,
        grid=(b,),
        in_specs=[pl.BlockSpec((None, l, GLA_KW), lambda bi: (bi, 0, ZB_GQ)),
                  pl.BlockSpec((None, l, GLA_KW), lambda bi: (bi, 0, ZB_GK)),
                  pl.BlockSpec((None, l, GLA_VW), lambda bi: (bi, 0, ZB_GV)),
                  pl.BlockSpec((None, l, GLA_VW), lambda bi: (bi, 0, ZB_GG)),
                  pl.BlockSpec((None, l, LANES), lambda bi: (bi, 0, 0)),
                  pl.BlockSpec((LANES, 2 * GLA_KW), lambda bi: (0, 0)),
                  pl.BlockSpec((1, 2 * GLA_KW), lambda bi: (0, 0)),
                  pl.BlockSpec((1, GLA_DV), lambda bi: (0, 0)),
                  pl.BlockSpec((None,) + st_shape, lambda bi: (bi, 0, 0, 0))],
        out_specs=[pl.BlockSpec((None, l, GLA_VW), lambda bi: (bi, 0, 0)),
                   pl.BlockSpec((None,) + st_shape, lambda bi: (bi, 0, 0, 0))],
        out_shape=[jax.ShapeDtypeStruct((b, l, GLA_VW), BF16),
                   jax.ShapeDtypeStruct((b,) + st_shape, F32)],
        scratch_shapes=[pltpu.VMEM((l, GLA_VW), F32),
                        pltpu.VMEM((2, l, GLA_KW), BF16),
                        pltpu.VMEM((2, n_chunks) + st_shape[1:], F32),
                        pltpu.VMEM((2, n_chunks // cg, st_shape[1], cg * GLA_CHUNK), F32),
                        pltpu.VMEM((2, n_chunks) + st_shape[1:], BF16),
                        pltpu.VMEM(st_shape, F32)],
        compiler_params=_cparams(("arbitrary",)),
        name="gla",
    )(z3, z3, z3, z3, ga3, aup, ab, gnorm.reshape(1, GLA_DV), s0)


def _merge_kernel(ya_ref, yb_ref, yc_ref, g0_ref, g1_ref, g2_ref, x_ref, gate_ref, sh_ref, sc_ref,
                  gpost_ref, gpre_ref, wa_ref, wb_ref, wc_ref, wo_ref, xo_ref, h_ref):
    ya = jnp.dot(ya_ref[...], wa_ref[...], preferred_element_type=F32)
    yb = jnp.dot(yb_ref[...], wb_ref[...], preferred_element_type=F32)
    yc = jnp.dot(yc_ref[...], wc_ref[...], preferred_element_type=F32)
    mix = (_sigmoid(g0_ref[...].astype(F32)) * ya + _sigmoid(g1_ref[...].astype(F32)) * yb
           + _sigmoid(g2_ref[...].astype(F32)) * yc)
    y = jnp.dot(mix.astype(BF16), wo_ref[...], preferred_element_type=F32)
    xn = x_ref[...] + gate_ref[...] * _rms(y, gpost_ref[...])
    xo_ref[...] = xn
    h_ref[...] = (_rms(xn, gpre_ref[...]) * (1.0 + sc_ref[...]) + sh_ref[...]).astype(BF16)


def _merge_call(ya, yb, yc, z, x2, mod3, g_post, g_pre, wa, wb, wc, wo, rows_per_mod, mod_row0, tm):
    t, d = x2.shape
    modspec = lambda k: pl.BlockSpec((None, 1, d), lambda i: (mod_row0 + i // rows_per_mod, 0, k))
    full = lambda a: pl.BlockSpec(a.shape, lambda i: (0,) * a.ndim)
    row = lambda w: pl.BlockSpec((tm, w), lambda i: (i, 0))
    in_specs = [row(DA_WIDTH), row(CV_WIDTH), row(GLA_VW),
                pl.BlockSpec((tm, d), lambda i: (i, ZB_GATES)),
                pl.BlockSpec((tm, d), lambda i: (i, ZB_GATES + 1)),
                pl.BlockSpec((tm, d), lambda i: (i, ZB_GATES + 2)),
                row(d), modspec(2), modspec(3), modspec(4),
                pl.BlockSpec((1, d), lambda i: (0, 0)), pl.BlockSpec((1, d), lambda i: (0, 0)),
                full(wa), full(wb), full(wc), full(wo)]
    return pl.pallas_call(
        _merge_kernel,
        grid=(t // tm,),
        in_specs=in_specs,
        out_specs=[row(d), row(d)],
        out_shape=[jax.ShapeDtypeStruct((t, d), F32), jax.ShapeDtypeStruct((t, d), BF16)],
        compiler_params=_cparams(("arbitrary",)),
        name="merge",
    )(ya, yb, yc, z, z, z, x2, mod3, mod3, mod3, g_post.reshape(1, d), g_pre.reshape(1, d), wa, wb, wc, wo)


FF_CHUNK = MXU_COLS


def _swiglu(h, w1_ref, w3_ref, w2_ref):
    acc = jnp.zeros((h.shape[0], w2_ref.shape[1]), F32)
    for c in range(0, w1_ref.shape[1], FF_CHUNK):
        u = jnp.dot(h, w1_ref[:, c:c + FF_CHUNK], preferred_element_type=F32)
        g = jnp.dot(h, w3_ref[:, c:c + FF_CHUNK], preferred_element_type=F32)
        a = (u * _sigmoid(u) * g).astype(BF16)
        acc = acc + jnp.dot(a, w2_ref[c:c + FF_CHUNK, :], preferred_element_type=F32)
    return acc


def _ffn_kernel(h_ref, x_ref, gate_ref, gpost_ref, w1_ref, w3_ref, w2_ref, xo_ref):
    f = _swiglu(h_ref[...], w1_ref, w3_ref, w2_ref)
    xo_ref[...] = x_ref[...] + gate_ref[...] * _rms(f, gpost_ref[...])


def _ffn_call(h2, x2, mod3, g_post, w1, w3, w2, rows_per_mod, mod_row0, tm):
    t, d = x2.shape
    full = lambda a: pl.BlockSpec(a.shape, lambda i: (0,) * a.ndim)
    row = pl.BlockSpec((tm, d), lambda i: (i, 0))
    return pl.pallas_call(
        _ffn_kernel,
        grid=(t // tm,),
        in_specs=[row, row,
                  pl.BlockSpec((None, 1, d), lambda i: (mod_row0 + i // rows_per_mod, 0, 5)),
                  pl.BlockSpec((1, d), lambda i: (0, 0)), full(w1), full(w3), full(w2)],
        out_specs=row,
        out_shape=jax.ShapeDtypeStruct((t, d), F32),
        compiler_params=_cparams(("arbitrary",)),
        name="ffn",
    )(h2, x2, mod3, g_post.reshape(1, d), w1, w3, w2)


def _top2_gate(logits):
    lane = lax.broadcasted_iota(jnp.int32, logits.shape, 1)
    neg = jnp.float32(-jnp.inf)
    lg = jnp.where(lane < N_EXPERTS, logits, neg)
    m1 = lg.max(axis=-1, keepdims=True)
    i1 = jnp.where(lg == m1, lane, LANES).min(axis=-1, keepdims=True)
    lg2 = jnp.where(lane == i1, neg, lg)
    m2 = lg2.max(axis=-1, keepdims=True)
    i2 = jnp.where(lg2 == m2, lane, LANES).min(axis=-1, keepdims=True)
    e2 = jnp.exp(m2 - m1)
    w1 = 1.0 / (1.0 + e2)
    gate = jnp.where(lane == i1, w1, jnp.where(lane == i2, e2 * w1, 0.0))
    sel = jnp.where(lane == i1, 1.0, jnp.where(lane == i2, 1.0, 0.0))
    return gate, sel


MOE_STEP = 64
MOE_SIZES = tuple(range(2 * MOE_STEP, 9 * MOE_STEP + 1, MOE_STEP))


def _moe_kernel(h_ref, rt_ref, w1_ref, w3_ref, w2_ref, f_ref, gate_scr, rank_scr):
    e = pl.program_id(1)
    tb = h_ref.shape[0]

    @pl.when(e == 0)
    def _():
        logits = jnp.dot(h_ref[...], rt_ref[...], preferred_element_type=F32)
        gate, sel = _top2_gate(logits)
        gate_scr[...] = gate
        r = lax.broadcasted_iota(jnp.int32, (tb, tb), 0)
        c = lax.broadcasted_iota(jnp.int32, (tb, tb), 1)
        before = jnp.where(c < r, 1.0, 0.0).astype(BF16)
        rank = jnp.dot(before, sel.astype(BF16), preferred_element_type=F32)
        rank_scr[...] = jnp.where(sel > 0.5, rank, -1.0).T
        f_ref[...] = jnp.zeros_like(f_ref)

    lane = lax.broadcasted_iota(jnp.int32, gate_scr.shape, 1)
    gcol = jnp.sum(jnp.where(lane == e, gate_scr[...], 0.0), axis=-1, keepdims=True)
    rank_row = rank_scr[pl.ds(e, 1), :]
    count = jnp.sum(jnp.where(rank_row >= 0.0, 1.0, 0.0))

    def expert_pass(start, rows):
        slot = lax.broadcasted_iota(jnp.int32, (rows, tb), 0).astype(F32) + float(start)
        onehot = jnp.where(rank_row == slot, 1.0, 0.0).astype(BF16)
        hs = jnp.dot(onehot, h_ref[...], preferred_element_type=F32).astype(BF16)
        y = _swiglu(hs, w1_ref, w3_ref, w2_ref).astype(BF16)
        back = lax.dot_general(onehot, y, (((0,), (0,)), ((), ())), preferred_element_type=F32)
        f_ref[...] += gcol * back

    sizes = [m for m in MOE_SIZES if m <= tb] or [tb]
    for idx, m in enumerate(sizes):
        lo = sizes[idx - 1] if idx else 0
        fits = (count > lo) if idx == len(sizes) - 1 else (count > lo) & (count <= m)
        pl.when(fits)(functools.partial(expert_pass, 0, m))
    if sizes[-1] < tb:
        pl.when(count > sizes[-1])(functools.partial(expert_pass, sizes[-1], tb - sizes[-1]))


def _moe_call(h2, router_p, w1, w3, w2, tb):
    t, d = h2.shape
    ne, _, ff = w1.shape
    row = pl.BlockSpec((tb, d), lambda i, e: (i, 0))
    return pl.pallas_call(
        _moe_kernel,
        grid=(t // tb, ne),
        in_specs=[row,
                  pl.BlockSpec((d, LANES), lambda i, e: (0, 0)),
                  pl.BlockSpec((None, d, ff), lambda i, e: (e, 0, 0)),
                  pl.BlockSpec((None, d, ff), lambda i, e: (e, 0, 0)),
                  pl.BlockSpec((None, ff, d), lambda i, e: (e, 0, 0))],
        out_specs=row,
        out_shape=jax.ShapeDtypeStruct((t, d), F32),
        scratch_shapes=[pltpu.VMEM((tb, LANES), F32), pltpu.VMEM((LANES, tb), F32)],
        compiler_params=_cparams(("arbitrary", "arbitrary")),
        name="moe",
    )(h2, router_p, w1, w3, w2)


def _resid_kernel(f_ref, x_ref, gate_ref, gpost_ref, xo_ref):
    xo_ref[...] = x_ref[...] + gate_ref[...] * _rms(f_ref[...], gpost_ref[...])


def _resid_call(f, x2, mod3, g_post, rows_per_mod, mod_row0, tm):
    t, d = x2.shape
    row = pl.BlockSpec((tm, d), lambda i: (i, 0))
    return pl.pallas_call(
        _resid_kernel,
        grid=(t // tm,),
        in_specs=[row, row,
                  pl.BlockSpec((None, 1, d), lambda i: (mod_row0 + i // rows_per_mod, 0, 5)),
                  pl.BlockSpec((1, d), lambda i: (0, 0))],
        out_specs=row,
        out_shape=jax.ShapeDtypeStruct((t, d), F32),
        compiler_params=_cparams(("arbitrary",)),
        name="resid",
    )(f, x2, mod3, g_post.reshape(1, d))


def _prep_w_in(w):
    o_q, o_k, o_v, o_c, o_gq, o_gk, o_gv, o_gg, o_ga, o_gt = 0, 512, 1024, 1536, 2560, 2816, 3072, 3584, 4096, 4128
    perm = (jnp.arange(DA_WIDTH // LANES)[:, None] * LANES + _qk_lane_perm()[None, :]).reshape(-1)
    main = jnp.concatenate([w[:, o_q:o_k][:, perm], w[:, o_k:o_v][:, perm], w[:, o_v:o_c], w[:, o_gv:o_gg], w[:, o_c:o_gq],
                            w[:, o_gg:o_ga], w[:, o_gq:o_gk], w[:, o_gk:o_gv], w[:, o_gt:]], axis=1)
    ga = jnp.pad(w[:, o_ga:o_gt], ((0, 0), (0, LANES - 2 * GLA_RANK)))
    return main.astype(BF16), ga.astype(BF16)


def _prep_a_up(a_up, a_b):
    m = jnp.zeros((LANES, 2 * GLA_KW), F32)
    m = m.at[:GLA_RANK, :GLA_KW].set(a_up[0]).at[GLA_RANK:2 * GLA_RANK, GLA_KW:].set(a_up[1])
    return m.astype(BF16), a_b.reshape(1, 2 * GLA_KW).astype(F32)


def _row_tile(n, pref):
    t = min(n, pref)
    assert n % t == 0
    return t


def kernel(x, c, ctx, c_ctx, ada_w, ada_b, g_mix_pre, g_mix_post, g_ffn_pre, g_ffn_post, w_in, da_lambda, da_subln, da_proj, cv_dw, cv_dw_b, cv_ln_g, cv_ln_b, cv_proj, gla_a_up, gla_a_b, gla_norm, gla_proj, w_out, ffn_w1, ffn_w3, ffn_w2, moe_router, moe_w1, moe_w3, moe_w2):
    bsz, seq, d = x.shape
    lc = ctx.shape[1]
    depth = w_in.shape[0]
    assert bsz + 1 <= MOD_ROWS and d == D_MODEL
    t_lat, t_ctx = bsz * seq, bsz * lc

    cc = jnp.zeros((MOD_ROWS, d), F32).at[:bsz].set(c).at[bsz].set(c_ctx)
    mod_all = _mod_call(cc, ada_w, ada_b)
    rope_tabs = _rope_tables(seq)

    tm_lat = _row_tile(seq, 2048)
    tm_ctx = _row_tile(t_ctx, 1024)
    tm_mix = _row_tile(seq, 512)
    tm_mix_ctx = _row_tile(t_ctx, 512)
    tq = _row_tile(seq, 256)
    tq_ctx = _row_tile(lc, 128)
    tc = _row_tile(seq, 256)
    tc_ctx = _row_tile(lc, 256)

    xl = x.reshape(t_lat, d)
    xc = ctx.reshape(t_ctx, d)
    for l in range(depth):
        need_ctx = l < depth - 1
        lambda_init = 0.8 - 0.6 * math.exp(-0.3 * l)
        mod3 = mod_all[l].reshape(MOD_ROWS, 1, 6 * d)
        w_main, w_ga = _prep_w_in(w_in[l])
        aup, ab = _prep_a_up(gla_a_up[l], gla_a_b[l])
        wa, wb, wc, wo = (da_proj[l].astype(BF16), cv_proj[l].astype(BF16),
                          gla_proj[l].astype(BF16), w_out[l].astype(BF16))

        z, ga = _inproj_call(xl, g_mix_pre[l], mod3, w_main, w_ga, rope_tabs, seq // tm_lat, 0, tm_lat)
        zc, gac = _inproj_call(xc, g_mix_pre[l], mod3, w_main, w_ga, None, t_ctx, bsz, tm_ctx)
        z3, zc3 = z.reshape(bsz, seq, Z_WIDTH), zc.reshape(bsz, lc, Z_WIDTH)

        ya = _attn_call(da_lambda[l], da_subln[l], z3, [z3, zc3], lambda_init, tq)
        yb = _conv_call(z3, cv_dw[l], cv_dw_b[l], cv_ln_g[l], cv_ln_b[l], tc)
        s_zero = jnp.zeros((bsz, 2, GLA_HEADS * GLA_DK, GLA_DV), F32)
        ycc, s_ctx = _gla_call(zc3, gac.reshape(bsz, lc, LANES), aup, ab, gla_norm[l], s_zero)
        yc, _ = _gla_call(z3, ga.reshape(bsz, seq, LANES), aup, ab, gla_norm[l], s_ctx)

        xl, h2 = _merge_call(ya.reshape(t_lat, -1), yb.reshape(t_lat, -1), yc.reshape(t_lat, -1), z, xl, mod3,
                             g_mix_post[l], g_ffn_pre[l], wa, wb, wc, wo, seq // tm_mix, 0, tm_mix)
        if need_ctx:
            yac = _attn_call(da_lambda[l], da_subln[l], zc3, [zc3], lambda_init, tq_ctx)
            ybc = _conv_call(zc3, cv_dw[l], cv_dw_b[l], cv_ln_g[l], cv_ln_b[l], tc_ctx)
            xc, hc2 = _merge_call(yac.reshape(t_ctx, -1), ybc.reshape(t_ctx, -1), ycc.reshape(t_ctx, -1), zc, xc,
                                  mod3, g_mix_post[l], g_ffn_pre[l], wa, wb, wc, wo, t_ctx, bsz, tm_mix_ctx)

        i = l // 2
        if l % 2 == 0:
            w1, w3, w2 = ffn_w1[i].astype(BF16), ffn_w3[i].astype(BF16), ffn_w2[i].astype(BF16)
            xl = _ffn_call(h2, xl, mod3, g_ffn_post[l], w1, w3, w2, seq // tm_mix, 0, tm_mix)
            if need_ctx:
                xc = _ffn_call(hc2, xc, mod3, g_ffn_post[l], w1, w3, w2, t_ctx, bsz, tm_mix_ctx)
        else:
            router_p = jnp.pad(moe_router[i], ((0, 0), (0, LANES - N_EXPERTS))).astype(BF16)
            w1, w3, w2 = moe_w1[i].astype(BF16), moe_w3[i].astype(BF16), moe_w2[i].astype(BF16)
            f = _moe_call(h2, router_p, w1, w3, w2, _row_tile(t_lat, 1024))
            xl = _resid_call(f, xl, mod3, g_ffn_post[l], seq // tm_mix, 0, tm_mix)
            if need_ctx:
                fc = _moe_call(hc2, router_p, w1, w3, w2, _row_tile(t_ctx, 1024))
                xc = _resid_call(fc, xc, mod3, g_ffn_post[l], t_ctx, bsz, tm_mix_ctx)
    return xl.reshape(bsz, seq, d)
```

```python
import functools
import math

import jax
import jax.numpy as jnp
from jax import lax
from jax.experimental import pallas as pl
from jax.experimental.pallas import tpu as pltpu

F32 = jnp.float32
BF16 = jnp.bfloat16
HIGHEST = lax.Precision.HIGHEST

D_MODEL = 1024
GRID_W = 64
DA_HEADS = 4
DA_HEAD_DIM = 64
DA_WIDTH = 512
ROPE_BASE = 10000.0
CV_WIDTH = 512
CV_KERNEL = 31
GLA_HEADS = 4
GLA_DK = 64
GLA_DV = 128
GLA_KW = 256
GLA_VW = 512
GLA_RANK = 16
GLA_NORMALIZER = 16.0
GLA_CHUNK = 64
GLA_GROUP = 4
N_EXPERTS = 8
D_FF = 2816
EPS = 1e-6
LOG2E = 1.4426950408889634

V7X_VMEM_BYTES = 64 * 1024 * 1024
VMEM_LIMIT = V7X_VMEM_BYTES - 8 * 1024 * 1024
LANES = 128
SUBLANES = 8
MXU_COLS = 256

Z_WIDTH = 7168
ZB_Q, ZB_K, ZB_V, ZB_GV = 0, 1, 2, 3
ZB_CIN = 2
ZB_GG = 6
ZB_GQ, ZB_GK = 14, 15
ZB_GATES = 4
MOD_ROWS = 40


def _cparams(sem, vmem=VMEM_LIMIT):
    return pltpu.CompilerParams(dimension_semantics=sem, vmem_limit_bytes=vmem)


def _sigmoid(x):
    return 1.0 / (1.0 + jnp.exp(-x))


def _rms(x, g):
    return x * lax.rsqrt(jnp.mean(x * x, axis=-1, keepdims=True) + EPS) * g


def _mod_kernel(c_ref, w_ref, b_ref, o_ref):
    c = c_ref[...]
    s = c * _sigmoid(c)
    o_ref[...] = jnp.dot(s, w_ref[...], preferred_element_type=F32, precision=HIGHEST) + b_ref[...]


def _mod_call(cc, ada_w, ada_b):
    depth, d, n6 = ada_w.shape
    tn = 1536
    return pl.pallas_call(
        _mod_kernel,
        grid=(depth, n6 // tn),
        in_specs=[pl.BlockSpec((MOD_ROWS, d), lambda l, j: (0, 0)),
                  pl.BlockSpec((None, d, tn), lambda l, j: (l, 0, j)),
                  pl.BlockSpec((None, 1, tn), lambda l, j: (l, 0, j))],
        out_specs=pl.BlockSpec((None, MOD_ROWS, tn), lambda l, j: (l, 0, j)),
        out_shape=jax.ShapeDtypeStruct((depth, MOD_ROWS, n6), F32),
        compiler_params=_cparams(("arbitrary", "arbitrary")),
        name="mod",
    )(cc, ada_w, ada_b.reshape(depth, 1, n6))


def _qk_lane_perm():
    n = jnp.arange(LANES)
    half, m = n // 64, n % 64
    return (m // 32) * 64 + ((m % 32) // 16) * 32 + half * 16 + m % 16


def _swap_halves(x):
    return pltpu.roll(x, LANES // 2, 1)


def _inproj_kernel(*refs, rope):
    if rope:
        (x_ref, g_ref, sh_ref, sc_ref, w_ref, wga_ref, cq_ref, sq_ref, ck_ref, sk_ref,
         z_ref, ga_ref, h_scr) = refs
    else:
        x_ref, g_ref, sh_ref, sc_ref, w_ref, wga_ref, z_ref, ga_ref, h_scr = refs
    j = pl.program_id(1)

    @pl.when(j == 0)
    def _():
        h = _rms(x_ref[...], g_ref[...]) * (1.0 + sc_ref[...]) + sh_ref[...]
        hb = h.astype(BF16)
        h_scr[...] = hb
        ga_ref[...] = jnp.dot(hb, wga_ref[...], preferred_element_type=F32).astype(BF16)

    acc = jnp.dot(h_scr[...], w_ref[...], preferred_element_type=F32)

    @pl.when(j == 0)
    def _():
        qscale = DA_HEAD_DIM ** -0.5 * LOG2E
        if rope:
            cols = []
            for c in range(acc.shape[1] // LANES):
                xs = acc[:, c * LANES:(c + 1) * LANES]
                is_q = c < DA_WIDTH // LANES
                cos = cq_ref[...] if is_q else ck_ref[...]
                sin = sq_ref[...] if is_q else sk_ref[...]
                cols.append((xs * cos + _swap_halves(xs) * sin).astype(BF16))
            z_ref[...] = jnp.concatenate(cols, axis=1)
        else:
            z_ref[:, :DA_WIDTH] = (acc[:, :DA_WIDTH] * qscale).astype(BF16)
            z_ref[:, DA_WIDTH:] = acc[:, DA_WIDTH:].astype(BF16)

    @pl.when(j != 0)
    def _():
        z_ref[...] = acc.astype(BF16)


def _inproj_call(x2, g, mod3, w_main, w_ga, rope_tabs, rows_per_mod, mod_row0, tm):
    t, d = x2.shape
    tn = 1024
    rope = rope_tabs is not None
    mrow = lambda i, j: (mod_row0 + i // rows_per_mod, 0, 0)
    in_specs = [pl.BlockSpec((tm, d), lambda i, j: (i, 0)),
                pl.BlockSpec((1, d), lambda i, j: (0, 0)),
                pl.BlockSpec((None, 1, d), mrow),
                pl.BlockSpec((None, 1, d), lambda i, j: (mod_row0 + i // rows_per_mod, 0, 1)),
                pl.BlockSpec((d, tn), lambda i, j: (0, j)),
                pl.BlockSpec((d, LANES), lambda i, j: (0, 0))]
    args = [x2, g.reshape(1, d), mod3, mod3, w_main, w_ga]
    if rope:
        nseq = rope_tabs[0].shape[0] // tm
        for tab in rope_tabs:
            in_specs.append(pl.BlockSpec((tm, LANES), lambda i, j: (i % nseq, 0)))
            args.append(tab)
    return pl.pallas_call(
        functools.partial(_inproj_kernel, rope=rope),
        grid=(t // tm, Z_WIDTH // tn),
        in_specs=in_specs,
        out_specs=[pl.BlockSpec((tm, tn), lambda i, j: (i, j)),
                   pl.BlockSpec((tm, LANES), lambda i, j: (i, 0))],
        out_shape=[jax.ShapeDtypeStruct((t, Z_WIDTH), BF16),
                   jax.ShapeDtypeStruct((t, LANES), BF16)],
        scratch_shapes=[pltpu.VMEM((tm, d), BF16)],
        compiler_params=_cparams(("arbitrary", "arbitrary")),
        name="inproj_rope" if rope else "inproj",
    )(*args)


def _rope_tables(seq):
    t = jnp.arange(seq)
    nf = DA_HEAD_DIM // 4
    inv = ROPE_BASE ** (-jnp.arange(nf, dtype=F32) / nf)
    ang_r = (t // GRID_W).astype(F32)[:, None] * inv
    ang_c = (t % GRID_W).astype(F32)[:, None] * inv
    cos32 = jnp.concatenate([jnp.cos(ang_r), jnp.cos(ang_c)], axis=1)
    sin32 = jnp.concatenate([jnp.sin(ang_r), jnp.sin(ang_c)], axis=1)
    cos = jnp.tile(cos32, (1, LANES // 32))
    sin = jnp.concatenate([-sin32, -sin32, sin32, sin32], axis=1)
    qscale = DA_HEAD_DIM ** -0.5 * LOG2E
    return cos * qscale, sin * qscale, cos, sin


ATT_ROWS = 16


def _attn_kernel(*refs, n_src, lambda_init):
    lam_ref, subln_ref, q_ref = refs[:3]
    kv_refs = refs[3:3 + 2 * n_src]
    o_ref = refs[3 + 2 * n_src]
    tq = q_ref.shape[0]
    hw = 2 * DA_HEAD_DIM

    lp = lam_ref[...]
    lam = (jnp.exp(jnp.sum(lp[0:1] * lp[1:2], axis=-1, keepdims=True))
           - jnp.exp(jnp.sum(lp[2:3] * lp[3:4], axis=-1, keepdims=True)) + lambda_init)
    lane = lax.broadcasted_iota(jnp.int32, (tq, hw), 1)
    first = (lane % 64) < 32

    for h in range(DA_HEADS):
        qh = q_ref[:, h * hw:(h + 1) * hw]
        zero = jnp.zeros_like(qh)
        qq = jnp.concatenate([jnp.where(first, qh, zero), jnp.where(first, zero, qh)], axis=0)
        scores = []
        for s in range(n_src):
            kh = kv_refs[2 * s][:, h * hw:(h + 1) * hw]
            scores.append(lax.dot_general(qq, kh, (((1,), (1,)), ((), ())),
                                          preferred_element_type=F32))
        probs = [[] for _ in range(n_src)]
        recips = []
        for rb in range(0, 2 * tq, ATT_ROWS):
            parts = [sc[rb:rb + ATT_ROWS, :] for sc in scores]
            m = parts[0].max(axis=-1, keepdims=True)
            for part in parts[1:]:
                m = jnp.maximum(m, part.max(axis=-1, keepdims=True))
            denom = None
            for s in range(n_src):
                p = jnp.exp2(parts[s] - m)
                ps = p.sum(axis=-1, keepdims=True)
                denom = ps if denom is None else denom + ps
                probs[s].append(p.astype(BF16))
            recips.append(1.0 / denom)
        r = jnp.concatenate(recips, axis=0)
        o0 = jnp.zeros((tq, hw), F32)
        o1 = jnp.zeros((tq, hw), F32)
        for s in range(n_src):
            pb = jnp.concatenate(probs[s], axis=0)
            vh = kv_refs[2 * s + 1][:, h * hw:(h + 1) * hw]
            o0 = o0 + jnp.dot(pb[:tq], vh, preferred_element_type=F32)
            o1 = o1 + jnp.dot(pb[tq:], vh, preferred_element_type=F32)
        o = o0 * r[:tq] - (lam * r[tq:]) * o1
        o = _rms(o, subln_ref[...]) * (1.0 - lambda_init)
        o_ref[:, h * hw:(h + 1) * hw] = o.astype(BF16)


def _attn_call(da_lambda, subln, zq3, kv_srcs, lambda_init, tq):
    b, lq, _ = zq3.shape
    in_specs = [pl.BlockSpec((4, DA_HEAD_DIM), lambda bi, i: (0, 0)),
                pl.BlockSpec((1, 2 * DA_HEAD_DIM), lambda bi, i: (0, 0)),
                pl.BlockSpec((None, tq, DA_WIDTH), lambda bi, i: (bi, i, ZB_Q))]
    args = [da_lambda, subln.reshape(1, -1), zq3]
    for src in kv_srcs:
        lk = src.shape[1]
        in_specs.append(pl.BlockSpec((None, lk, DA_WIDTH), lambda bi, i: (bi, 0, ZB_K)))
        in_specs.append(pl.BlockSpec((None, lk, DA_WIDTH), lambda bi, i: (bi, 0, ZB_V)))
        args += [src, src]
    return pl.pallas_call(
        functools.partial(_attn_kernel, n_src=len(kv_srcs), lambda_init=lambda_init),
        grid=(b, lq // tq),
        in_specs=in_specs,
        out_specs=pl.BlockSpec((None, tq, DA_WIDTH), lambda bi, i: (bi, i, 0)),
        out_shape=jax.ShapeDtypeStruct((b, lq, DA_WIDTH), BF16),
        compiler_params=_cparams(("arbitrary", "arbitrary")),
        name="attn",
    )(*args)


CV_HALO = 16
CV_ROWS = 32


def _conv_kernel(prev_ref, main_ref, next_ref, dw_ref, dwb_ref, lng_ref, lnb_ref, o_ref, hs_scr, sh_scr):
    i = pl.program_id(1)
    n = pl.num_programs(1)
    tm = main_ref.shape[0]

    def glu(u):
        a = u[:, :CV_WIDTH].astype(F32)
        g = u[:, CV_WIDTH:].astype(F32)
        return a * _sigmoid(g)

    hs_scr[0:CV_HALO, :] = jnp.where(i > 0, glu(prev_ref[...]), 0.0)
    hs_scr[CV_HALO:CV_HALO + tm, :] = glu(main_ref[...])
    hs_scr[CV_HALO + tm:2 * CV_HALO + tm, :] = jnp.where(i < n - 1, glu(next_ref[...]), 0.0)

    span = tm + 2 * CV_HALO - SUBLANES
    for s in range(SUBLANES):
        sh_scr[s, 0:span, :] = hs_scr[s:s + span, :]

    off = CV_HALO - CV_KERNEL // 2
    for r in range(0, tm, CV_ROWS):
        acc = jnp.zeros((CV_ROWS, CV_WIDTH), F32) + dwb_ref[...]
        for j in range(CV_KERNEL):
            s, a = (off + j) % SUBLANES, (off + j) // SUBLANES * SUBLANES
            acc = acc + sh_scr[s, r + a:r + a + CV_ROWS, :] * dw_ref[j:j + 1, :]
        mu = jnp.mean(acc, axis=-1, keepdims=True)
        xc = acc - mu
        var = jnp.mean(xc * xc, axis=-1, keepdims=True)
        y = xc * lax.rsqrt(var + EPS) * lng_ref[...] + lnb_ref[...]
        o_ref[r:r + CV_ROWS, :] = (y * _sigmoid(y)).astype(BF16)


def _conv_call(z3, dw, dwb, lng, lnb, tm):
    b, l, _ = z3.shape
    nh = tm // CV_HALO
    last = l // CV_HALO - 1
    cw = 2 * CV_WIDTH
    vec = lambda v: v.reshape(1, CV_WIDTH)
    return pl.pallas_call(
        _conv_kernel,
        grid=(b, l // tm),
        in_specs=[pl.BlockSpec((None, CV_HALO, cw), lambda bi, i: (bi, jnp.maximum(i * nh - 1, 0), ZB_CIN)),
                  pl.BlockSpec((None, tm, cw), lambda bi, i: (bi, i, ZB_CIN)),
                  pl.BlockSpec((None, CV_HALO, cw), lambda bi, i: (bi, jnp.minimum((i + 1) * nh, last), ZB_CIN)),
                  pl.BlockSpec((CV_KERNEL, CV_WIDTH), lambda bi, i: (0, 0)),
                  pl.BlockSpec((1, CV_WIDTH), lambda bi, i: (0, 0)),
                  pl.BlockSpec((1, CV_WIDTH), lambda bi, i: (0, 0)),
                  pl.BlockSpec((1, CV_WIDTH), lambda bi, i: (0, 0))],
        out_specs=pl.BlockSpec((None, tm, CV_WIDTH), lambda bi, i: (bi, i, 0)),
        out_shape=jax.ShapeDtypeStruct((b, l, CV_WIDTH), BF16),
        scratch_shapes=[pltpu.VMEM((tm + 2 * CV_HALO, CV_WIDTH), F32),
                        pltpu.VMEM((SUBLANES, tm + 2 * CV_HALO, CV_WIDTH), F32)],
        compiler_params=_cparams(("arbitrary", "arbitrary")),
        name="conv",
    )(z3, z3, z3, dw, vec(dwb), vec(lng), vec(lnb))


def _exact_dot01(t01, x):
    hi = x.astype(BF16)
    r1 = x - hi.astype(F32)
    mid = r1.astype(BF16)
    lo = (r1 - mid.astype(F32)).astype(BF16)
    dot = lambda v: jnp.dot(t01, v, preferred_element_type=F32)
    return dot(hi) + dot(mid) + dot(lo)


def _gla_kernel(q_ref, k_ref, v_ref, gg_ref, ga_ref, aup_ref, ab_ref, gn_ref, s0_ref,
                y_ref, sfin_ref, o_scr, qd_scr, ds_scr, dt_scr, sin_scr, st_scr):
    seq = q_ref.shape[0]
    ck = GLA_CHUNK
    n_chunks = seq // ck
    cg = min(GLA_GROUP, n_chunks)
    rows = cg * ck
    n_groups = n_chunks // cg
    nh = GLA_HEADS

    ri = lax.broadcasted_iota(jnp.int32, (rows, rows), 0)
    ci = lax.broadcasted_iota(jnp.int32, (rows, rows), 1)
    same_chunk = (ri // ck) == (ci // ck)
    keep = (same_chunk & (ri >= ci), same_chunk & (ri <= ci))
    keep_bf = tuple(jnp.where(m, 1.0, 0.0).astype(BF16) for m in keep)
    lane_head = lax.broadcasted_iota(jnp.int32, (rows, GLA_KW), 1) // GLA_DK
    hrow = lax.broadcasted_iota(jnp.int32, (nh * ck, GLA_KW), 0) // ck
    hlane = lax.broadcasted_iota(jnp.int32, (nh * ck, GLA_KW), 1) // GLA_DK
    head_mask = hrow == hlane

    def phase_a(g, carry):
        r0 = pl.multiple_of(g * rows, rows)
        x = jnp.dot(ga_ref[pl.ds(r0, rows), :], aup_ref[...], preferred_element_type=F32) + ab_ref[...]
        la2 = (jnp.minimum(x, 0.0) - jnp.log(1.0 + jnp.exp(-jnp.abs(x)))) * (1.0 / GLA_NORMALIZER)
        q = q_ref[pl.ds(r0, rows), :].astype(F32) * (GLA_DK ** -0.5)
        k = k_ref[pl.ds(r0, rows), :].astype(F32)
        v = v_ref[pl.ds(r0, rows), :]
        o_parts = [jnp.zeros((rows, GLA_DV), F32) for _ in range(nh)]
        for d in range(2):
            la = la2[:, d * GLA_KW:(d + 1) * GLA_KW]
            b = _exact_dot01(keep_bf[d], la)
            last = ck - 1 if d == 0 else 0
            b_last = jnp.concatenate(
                [jnp.broadcast_to(b[c * ck + last:c * ck + last + 1, :], (ck, GLA_KW)) for c in range(cg)], axis=0)
            q_dec = (q * jnp.exp(b)).astype(BF16)
            k_inv = (k * jnp.exp(-b)).astype(BF16)
            kt_t = (k * jnp.exp(b_last - b)).T.astype(BF16)
            qd_scr[d, pl.ds(r0, rows), :] = q_dec
            dt_scr[d, g] = jnp.exp(b_last).T
            for h in range(nh):
                qm = jnp.where(lane_head == h, q_dec, jnp.zeros_like(q_dec))
                sc = lax.dot_general(qm, k_inv, (((1,), (1,)), ((), ())), preferred_element_type=F32)
                sc = jnp.where(keep[d], sc, 0.0).astype(BF16)
                vh = v[:, h * GLA_DV:(h + 1) * GLA_DV]
                o_parts[h] = o_parts[h] + jnp.dot(sc, vh, preferred_element_type=F32)
                kth = jnp.concatenate([kt_t[h * GLA_DK:(h + 1) * GLA_DK, :]] * cg, axis=0)
                lhs = jnp.where(same_chunk, kth, jnp.zeros_like(kth))
                ds = jnp.dot(lhs, vh, preferred_element_type=F32)
                for c in range(cg):
                    ds_scr[d, g * cg + c, h * GLA_DK:(h + 1) * GLA_DK, :] = ds[c * ck:(c + 1) * ck, :]
        o_scr[pl.ds(r0, rows), :] = jnp.concatenate(o_parts, axis=1)
        return carry

    lax.fori_loop(0, n_groups, phase_a, 0, unroll=min(2, n_groups))

    st_scr[...] = s0_ref[...]

    def phase_b(i, carry):
        for d in range(2):
            g = i if d == 0 else n_groups - 1 - i
            dt = dt_scr[d, g]
            for j in range(cg):
                c = j if d == 0 else cg - 1 - j
                state = st_scr[d]
                sin_scr[d, g * cg + c] = state.astype(BF16)
                st_scr[d] = dt[:, c * ck:c * ck + 1] * state + ds_scr[d, g * cg + c]
        return carry

    lax.fori_loop(0, n_groups, phase_b, 0)
    sfin_ref[...] = st_scr[...]

    def phase_c(g, carry):
        r0 = pl.multiple_of(g * rows, rows)
        o = o_scr[pl.ds(r0, rows), :]
        inter = []
        for c in range(cg):
            rc = pl.multiple_of(r0 + c * ck, ck)
            acc = jnp.zeros((nh * ck, GLA_DV), F32)
            for d in range(2):
                qd = qd_scr[d, pl.ds(rc, ck), :]
                qm = jnp.where(head_mask, jnp.concatenate([qd] * nh, axis=0), jnp.zeros((nh * ck, GLA_KW), BF16))
                acc = acc + jnp.dot(qm, sin_scr[d, g * cg + c], preferred_element_type=F32)
            inter.append(jnp.concatenate([acc[h * ck:(h + 1) * ck, :] for h in range(nh)], axis=1))
        o = o + jnp.concatenate(inter, axis=0)
        gg = gg_ref[pl.ds(r0, rows), :].astype(F32)
        for h in range(nh):
            sl = slice(h * GLA_DV, (h + 1) * GLA_DV)
            gh = gg[:, sl]
            y_ref[pl.ds(r0, rows), sl] = (_rms(o[:, sl], gn_ref[...]) * (gh * _sigmoid(gh))).astype(BF16)
        return carry

    lax.fori_loop(0, n_groups, phase_c, 0, unroll=min(2, n_groups))


def _gla_call(z3, ga3, aup, ab, gnorm, s0):
    b, l, _ = z3.shape
    st_shape = (2, GLA_HEADS * GLA_DK, GLA_DV)
    n_chunks = l // GLA_CHUNK
    cg = min(GLA_GROUP, n_chunks)
    return pl.pallas_call(
        _gla_kernel,
        grid=(b,),
        in_specs=[pl.BlockSpec((None, l, GLA_KW), lambda bi: (bi, 0, ZB_GQ)),
                  pl.BlockSpec((None, l, GLA_KW), lambda bi: (bi, 0, ZB_GK)),
                  pl.BlockSpec((None, l, GLA_VW), lambda bi: (bi, 0, ZB_GV)),
                  pl.BlockSpec((None, l, GLA_VW), lambda bi: (bi, 0, ZB_GG)),
                  pl.BlockSpec((None, l, LANES), lambda bi: (bi, 0, 0)),
                  pl.BlockSpec((LANES, 2 * GLA_KW), lambda bi: (0, 0)),
                  pl.BlockSpec((1, 2 * GLA_KW), lambda bi: (0, 0)),
                  pl.BlockSpec((1, GLA_DV), lambda bi: (0, 0)),
                  pl.BlockSpec((None,) + st_shape, lambda bi: (bi, 0, 0, 0))],
        out_specs=[pl.BlockSpec((None, l, GLA_VW), lambda bi: (bi, 0, 0)),
                   pl.BlockSpec((None,) + st_shape, lambda bi: (bi, 0, 0, 0))],
        out_shape=[jax.ShapeDtypeStruct((b, l, GLA_VW), BF16),
                   jax.ShapeDtypeStruct((b,) + st_shape, F32)],
        scratch_shapes=[pltpu.VMEM((l, GLA_VW), F32),
                        pltpu.VMEM((2, l, GLA_KW), BF16),
                        pltpu.VMEM((2, n_chunks) + st_shape[1:], F32),
                        pltpu.VMEM((2, n_chunks // cg, st_shape[1], cg * GLA_CHUNK), F32),
                        pltpu.VMEM((2, n_chunks) + st_shape[1:], BF16),
                        pltpu.VMEM(st_shape, F32)],
        compiler_params=_cparams(("arbitrary",)),
        name="gla",
    )(z3, z3, z3, z3, ga3, aup, ab, gnorm.reshape(1, GLA_DV), s0)


def _merge_kernel(ya_ref, yb_ref, yc_ref, g0_ref, g1_ref, g2_ref, x_ref, gate_ref, sh_ref, sc_ref,
                  gpost_ref, gpre_ref, wa_ref, wb_ref, wc_ref, wo_ref, xo_ref, h_ref):
    ya = jnp.dot(ya_ref[...], wa_ref[...], preferred_element_type=F32)
    yb = jnp.dot(yb_ref[...], wb_ref[...], preferred_element_type=F32)
    yc = jnp.dot(yc_ref[...], wc_ref[...], preferred_element_type=F32)
    mix = (_sigmoid(g0_ref[...].astype(F32)) * ya + _sigmoid(g1_ref[...].astype(F32)) * yb
           + _sigmoid(g2_ref[...].astype(F32)) * yc)
    y = jnp.dot(mix.astype(BF16), wo_ref[...], preferred_element_type=F32)
    xn = x_ref[...] + gate_ref[...] * _rms(y, gpost_ref[...])
    xo_ref[...] = xn
    h_ref[...] = (_rms(xn, gpre_ref[...]) * (1.0 + sc_ref[...]) + sh_ref[...]).astype(BF16)


def _merge_call(ya, yb, yc, z, x2, mod3, g_post, g_pre, wa, wb, wc, wo, rows_per_mod, mod_row0, tm):
    t, d = x2.shape
    modspec = lambda k: pl.BlockSpec((None, 1, d), lambda i: (mod_row0 + i // rows_per_mod, 0, k))
    full = lambda a: pl.BlockSpec(a.shape, lambda i: (0,) * a.ndim)
    row = lambda w: pl.BlockSpec((tm, w), lambda i: (i, 0))
    in_specs = [row(DA_WIDTH), row(CV_WIDTH), row(GLA_VW),
                pl.BlockSpec((tm, d), lambda i: (i, ZB_GATES)),
                pl.BlockSpec((tm, d), lambda i: (i, ZB_GATES + 1)),
                pl.BlockSpec((tm, d), lambda i: (i, ZB_GATES + 2)),
                row(d), modspec(2), modspec(3), modspec(4),
                pl.BlockSpec((1, d), lambda i: (0, 0)), pl.BlockSpec((1, d), lambda i: (0, 0)),
                full(wa), full(wb), full(wc), full(wo)]
    return pl.pallas_call(
        _merge_kernel,
        grid=(t // tm,),
        in_specs=in_specs,
        out_specs=[row(d), row(d)],
        out_shape=[jax.ShapeDtypeStruct((t, d), F32), jax.ShapeDtypeStruct((t, d), BF16)],
        compiler_params=_cparams(("arbitrary",)),
        name="merge",
    )(ya, yb, yc, z, z, z, x2, mod3, mod3, mod3, g_post.reshape(1, d), g_pre.reshape(1, d), wa, wb, wc, wo)


FF_CHUNK = MXU_COLS


def _swiglu(h, w1_ref, w3_ref, w2_ref):
    acc = jnp.zeros((h.shape[0], w2_ref.shape[1]), F32)
    for c in range(0, w1_ref.shape[1], FF_CHUNK):
        u = jnp.dot(h, w1_ref[:, c:c + FF_CHUNK], preferred_element_type=F32)
        g = jnp.dot(h, w3_ref[:, c:c + FF_CHUNK], preferred_element_type=F32)
        a = (u * _sigmoid(u) * g).astype(BF16)
        acc = acc + jnp.dot(a, w2_ref[c:c + FF_CHUNK, :], preferred_element_type=F32)
    return acc


def _ffn_kernel(h_ref, x_ref, gate_ref, gpost_ref, w1_ref, w3_ref, w2_ref, xo_ref):
    f = _swiglu(h_ref[...], w1_ref, w3_ref, w2_ref)
    xo_ref[...] = x_ref[...] + gate_ref[...] * _rms(f, gpost_ref[...])


def _ffn_call(h2, x2, mod3, g_post, w1, w3, w2, rows_per_mod, mod_row0, tm):
    t, d = x2.shape
    full = lambda a: pl.BlockSpec(a.shape, lambda i: (0,) * a.ndim)
    row = pl.BlockSpec((tm, d), lambda i: (i, 0))
    return pl.pallas_call(
        _ffn_kernel,
        grid=(t // tm,),
        in_specs=[row, row,
                  pl.BlockSpec((None, 1, d), lambda i: (mod_row0 + i // rows_per_mod, 0, 5)),
                  pl.BlockSpec((1, d), lambda i: (0, 0)), full(w1), full(w3), full(w2)],
        out_specs=row,
        out_shape=jax.ShapeDtypeStruct((t, d), F32),
        compiler_params=_cparams(("arbitrary",)),
        name="ffn",
    )(h2, x2, mod3, g_post.reshape(1, d), w1, w3, w2)


def _top2_gate(logits):
    lane = lax.broadcasted_iota(jnp.int32, logits.shape, 1)
    neg = jnp.float32(-jnp.inf)
    lg = jnp.where(lane < N_EXPERTS, logits, neg)
    m1 = lg.max(axis=-1, keepdims=True)
    i1 = jnp.where(lg == m1, lane, LANES).min(axis=-1, keepdims=True)
    lg2 = jnp.where(lane == i1, neg, lg)
    m2 = lg2.max(axis=-1, keepdims=True)
    i2 = jnp.where(lg2 == m2, lane, LANES).min(axis=-1, keepdims=True)
    e2 = jnp.exp(m2 - m1)
    w1 = 1.0 / (1.0 + e2)
    gate = jnp.where(lane == i1, w1, jnp.where(lane == i2, e2 * w1, 0.0))
    sel = jnp.where(lane == i1, 1.0, jnp.where(lane == i2, 1.0, 0.0))
    return gate, sel


MOE_SMALL = MXU_COLS
MOE_LARGE = MXU_COLS * 3 // 2


def _moe_kernel(h_ref, rt_ref, w1_ref, w3_ref, w2_ref, f_ref, gate_scr, rank_scr):
    e = pl.program_id(1)
    tb = h_ref.shape[0]

    @pl.when(e == 0)
    def _():
        logits = jnp.dot(h_ref[...], rt_ref[...], preferred_element_type=F32)
        gate, sel = _top2_gate(logits)
        gate_scr[...] = gate
        r = lax.broadcasted_iota(jnp.int32, (tb, tb), 0)
        c = lax.broadcasted_iota(jnp.int32, (tb, tb), 1)
        before = jnp.where(c < r, 1.0, 0.0).astype(BF16)
        rank = jnp.dot(before, sel.astype(BF16), preferred_element_type=F32)
        rank_scr[...] = jnp.where(sel > 0.5, rank, -1.0).T
        f_ref[...] = jnp.zeros_like(f_ref)

    lane = lax.broadcasted_iota(jnp.int32, gate_scr.shape, 1)
    gcol = jnp.sum(jnp.where(lane == e, gate_scr[...], 0.0), axis=-1, keepdims=True)
    rank_row = rank_scr[pl.ds(e, 1), :]
    count = jnp.sum(jnp.where(rank_row >= 0.0, 1.0, 0.0))

    def expert_pass(start, rows):
        slot = lax.broadcasted_iota(jnp.int32, (rows, tb), 0).astype(F32) + start
        onehot = jnp.where(rank_row == slot, 1.0, 0.0).astype(BF16)
        hs = jnp.dot(onehot, h_ref[...], preferred_element_type=F32).astype(BF16)
        y = _swiglu(hs, w1_ref, w3_ref, w2_ref).astype(BF16)
        back = lax.dot_general(onehot, y, (((0,), (0,)), ((), ())), preferred_element_type=F32)
        f_ref[...] += gcol * back

    if tb <= MOE_SMALL:
        pl.when(count > 0)(functools.partial(expert_pass, 0, tb))
        return
    pl.when((count > 0) & (count <= MOE_SMALL))(functools.partial(expert_pass, 0, MOE_SMALL))
    pl.when(count > MOE_SMALL)(functools.partial(expert_pass, 0, MOE_LARGE))
    left = jnp.maximum(count.astype(jnp.int32) - MOE_LARGE, 0)
    n_tail = (left + MOE_SMALL - 1) // MOE_SMALL

    def tail(i, carry):
        expert_pass((MOE_LARGE + i * MOE_SMALL).astype(F32), MOE_SMALL)
        return carry

    lax.fori_loop(0, n_tail, tail, 0)


def _moe_call(h2, router_p, w1, w3, w2, tb):
    t, d = h2.shape
    ne, _, ff = w1.shape
    row = pl.BlockSpec((tb, d), lambda i, e: (i, 0))
    return pl.pallas_call(
        _moe_kernel,
        grid=(t // tb, ne),
        in_specs=[row,
                  pl.BlockSpec((d, LANES), lambda i, e: (0, 0)),
                  pl.BlockSpec((None, d, ff), lambda i, e: (e, 0, 0)),
                  pl.BlockSpec((None, d, ff), lambda i, e: (e, 0, 0)),
                  pl.BlockSpec((None, ff, d), lambda i, e: (e, 0, 0))],
        out_specs=row,
        out_shape=jax.ShapeDtypeStruct((t, d), F32),
        scratch_shapes=[pltpu.VMEM((tb, LANES), F32), pltpu.VMEM((LANES, tb), F32)],
        compiler_params=_cparams(("arbitrary", "arbitrary")),
        name="moe",
    )(h2, router_p, w1, w3, w2)


def _resid_kernel(f_ref, x_ref, gate_ref, gpost_ref, xo_ref):
    xo_ref[...] = x_ref[...] + gate_ref[...] * _rms(f_ref[...], gpost_ref[...])


def _resid_call(f, x2, mod3, g_post, rows_per_mod, mod_row0, tm):
    t, d = x2.shape
    row = pl.BlockSpec((tm, d), lambda i: (i, 0))
    return pl.pallas_call(
        _resid_kernel,
        grid=(t // tm,),
        in_specs=[row, row,
                  pl.BlockSpec((None, 1, d), lambda i: (mod_row0 + i // rows_per_mod, 0, 5)),
                  pl.BlockSpec((1, d), lambda i: (0, 0))],
        out_specs=row,
        out_shape=jax.ShapeDtypeStruct((t, d), F32),
        compiler_params=_cparams(("arbitrary",)),
        name="resid",
    )(f, x2, mod3, g_post.reshape(1, d))


def _prep_w_in(w):
    o_q, o_k, o_v, o_c, o_gq, o_gk, o_gv, o_gg, o_ga, o_gt = 0, 512, 1024, 1536, 2560, 2816, 3072, 3584, 4096, 4128
    perm = (jnp.arange(DA_WIDTH // LANES)[:, None] * LANES + _qk_lane_perm()[None, :]).reshape(-1)
    main = jnp.concatenate([w[:, o_q:o_k][:, perm], w[:, o_k:o_v][:, perm], w[:, o_v:o_c], w[:, o_gv:o_gg], w[:, o_c:o_gq],
                            w[:, o_gg:o_ga], w[:, o_gq:o_gk], w[:, o_gk:o_gv], w[:, o_gt:]], axis=1)
    ga = jnp.pad(w[:, o_ga:o_gt], ((0, 0), (0, LANES - 2 * GLA_RANK)))
    return main.astype(BF16), ga.astype(BF16)


def _prep_a_up(a_up, a_b):
    m = jnp.zeros((LANES, 2 * GLA_KW), F32)
    m = m.at[:GLA_RANK, :GLA_KW].set(a_up[0]).at[GLA_RANK:2 * GLA_RANK, GLA_KW:].set(a_up[1])
    return m.astype(BF16), a_b.reshape(1, 2 * GLA_KW).astype(F32)


def _row_tile(n, pref):
    t = min(n, pref)
    assert n % t == 0
    return t


def kernel(x, c, ctx, c_ctx, ada_w, ada_b, g_mix_pre, g_mix_post, g_ffn_pre, g_ffn_post, w_in, da_lambda, da_subln, da_proj, cv_dw, cv_dw_b, cv_ln_g, cv_ln_b, cv_proj, gla_a_up, gla_a_b, gla_norm, gla_proj, w_out, ffn_w1, ffn_w3, ffn_w2, moe_router, moe_w1, moe_w3, moe_w2):
    bsz, seq, d = x.shape
    lc = ctx.shape[1]
    depth = w_in.shape[0]
    assert bsz + 1 <= MOD_ROWS and d == D_MODEL
    t_lat, t_ctx = bsz * seq, bsz * lc

    cc = jnp.zeros((MOD_ROWS, d), F32).at[:bsz].set(c).at[bsz].set(c_ctx)
    mod_all = _mod_call(cc, ada_w, ada_b)
    rope_tabs = _rope_tables(seq)

    tm_lat = _row_tile(seq, 2048)
    tm_ctx = _row_tile(t_ctx, 1024)
    tm_mix = _row_tile(seq, 512)
    tm_mix_ctx = _row_tile(t_ctx, 512)
    tq = _row_tile(seq, 256)
    tq_ctx = _row_tile(lc, 128)
    tc = _row_tile(seq, 256)
    tc_ctx = _row_tile(lc, 256)

    xl = x.reshape(t_lat, d)
    xc = ctx.reshape(t_ctx, d)
    for l in range(depth):
        need_ctx = l < depth - 1
        lambda_init = 0.8 - 0.6 * math.exp(-0.3 * l)
        mod3 = mod_all[l].reshape(MOD_ROWS, 1, 6 * d)
        w_main, w_ga = _prep_w_in(w_in[l])
        aup, ab = _prep_a_up(gla_a_up[l], gla_a_b[l])
        wa, wb, wc, wo = (da_proj[l].astype(BF16), cv_proj[l].astype(BF16),
                          gla_proj[l].astype(BF16), w_out[l].astype(BF16))

        z, ga = _inproj_call(xl, g_mix_pre[l], mod3, w_main, w_ga, rope_tabs, seq // tm_lat, 0, tm_lat)
        zc, gac = _inproj_call(xc, g_mix_pre[l], mod3, w_main, w_ga, None, t_ctx, bsz, tm_ctx)
        z3, zc3 = z.reshape(bsz, seq, Z_WIDTH), zc.reshape(bsz, lc, Z_WIDTH)

        ya = _attn_call(da_lambda[l], da_subln[l], z3, [z3, zc3], lambda_init, tq)
        yb = _conv_call(z3, cv_dw[l], cv_dw_b[l], cv_ln_g[l], cv_ln_b[l], tc)
        s_zero = jnp.zeros((bsz, 2, GLA_HEADS * GLA_DK, GLA_DV), F32)
        ycc, s_ctx = _gla_call(zc3, gac.reshape(bsz, lc, LANES), aup, ab, gla_norm[l], s_zero)
        yc, _ = _gla_call(z3, ga.reshape(bsz, seq, LANES), aup, ab, gla_norm[l], s_ctx)

        xl, h2 = _merge_call(ya.reshape(t_lat, -1), yb.reshape(t_lat, -1), yc.reshape(t_lat, -1), z, xl, mod3,
                             g_mix_post[l], g_ffn_pre[l], wa, wb, wc, wo, seq // tm_mix, 0, tm_mix)
        if need_ctx:
            yac = _attn_call(da_lambda[l], da_subln[l], zc3, [zc3], lambda_init, tq_ctx)
            ybc = _conv_call(zc3, cv_dw[l], cv_dw_b[l], cv_ln_g[l], cv_ln_b[l], tc_ctx)
            xc, hc2 = _merge_call(yac.reshape(t_ctx, -1), ybc.reshape(t_ctx, -1), ycc.reshape(t_ctx, -1), zc, xc,
                                  mod3, g_mix_post[l], g_ffn_pre[l], wa, wb, wc, wo, t_ctx, bsz, tm_mix_ctx)

        i = l // 2
        if l % 2 == 0:
            w1, w3, w2 = ffn_w1[i].astype(BF16), ffn_w3[i].astype(BF16), ffn_w2[i].astype(BF16)
            xl = _ffn_call(h2, xl, mod3, g_ffn_post[l], w1, w3, w2, seq // tm_mix, 0, tm_mix)
            if need_ctx:
                xc = _ffn_call(hc2, xc, mod3, g_ffn_post[l], w1, w3, w2, t_ctx, bsz, tm_mix_ctx)
        else:
            router_p = jnp.pad(moe_router[i], ((0, 0), (0, LANES - N_EXPERTS))).astype(BF16)
            w1, w3, w2 = moe_w1[i].astype(BF16), moe_w3[i].astype(BF16), moe_w2[i].astype(BF16)
            f = _moe_call(h2, router_p, w1, w3, w2, _row_tile(t_lat, 1024))
            xl = _resid_call(f, xl, mod3, g_ffn_post[l], seq // tm_mix, 0, tm_mix)
            if need_ctx:
                fc = _moe_call(hc2, router_p, w1, w3, w2, _row_tile(t_ctx, 1024))
                xc = _resid_call(fc, xc, mod3, g_ffn_post[l], t_ctx, bsz, tm_mix_ctx)
    return xl.reshape(bsz, seq, d)
```

```python
import functools
import math

import jax
import jax.numpy as jnp
from jax import lax
from jax.experimental import pallas as pl
from jax.experimental.pallas import tpu as pltpu

F32 = jnp.float32
BF16 = jnp.bfloat16
HIGHEST = lax.Precision.HIGHEST

D_MODEL = 1024
GRID_W = 64
DA_HEADS = 4
DA_HEAD_DIM = 64
DA_WIDTH = 512
ROPE_BASE = 10000.0
CV_WIDTH = 512
CV_KERNEL = 31
GLA_HEADS = 4
GLA_DK = 64
GLA_DV = 128
GLA_KW = 256
GLA_VW = 512
GLA_RANK = 16
GLA_NORMALIZER = 16.0
GLA_CHUNK = 64
GLA_GROUP = 4
N_EXPERTS = 8
TOP_K = 2
D_FF = 2816
EPS = 1e-6
LOG2E = 1.4426950408889634

V7X_VMEM_BYTES = 64 * 1024 * 1024
VMEM_LIMIT = V7X_VMEM_BYTES - 8 * 1024 * 1024
LANES = 128
SUBLANES = 8
MXU_COLS = 256

Z_WIDTH = 7168
ZB_Q, ZB_K, ZB_V, ZB_GV = 0, 1, 2, 3
ZB_CIN = 2
ZB_GG = 6
ZB_GQ, ZB_GK = 14, 15
ZB_GATES = 4
MOD_ROWS = 40


def _cparams(sem, vmem=VMEM_LIMIT):
    return pltpu.CompilerParams(dimension_semantics=sem, vmem_limit_bytes=vmem)


def _sigmoid(x):
    return 1.0 / (1.0 + jnp.exp(-x))


def _rms(x, g):
    return x * lax.rsqrt(jnp.mean(x * x, axis=-1, keepdims=True) + EPS) * g


def _mod_kernel(c_ref, w_ref, b_ref, o_ref):
    c = c_ref[...]
    s = c * _sigmoid(c)
    o_ref[...] = jnp.dot(s, w_ref[...], preferred_element_type=F32, precision=HIGHEST) + b_ref[...]


def _mod_call(cc, ada_w, ada_b):
    depth, d, n6 = ada_w.shape
    tn = 1536
    return pl.pallas_call(
        _mod_kernel,
        grid=(depth, n6 // tn),
        in_specs=[pl.BlockSpec((MOD_ROWS, d), lambda l, j: (0, 0)),
                  pl.BlockSpec((None, d, tn), lambda l, j: (l, 0, j)),
                  pl.BlockSpec((None, 1, tn), lambda l, j: (l, 0, j))],
        out_specs=pl.BlockSpec((None, MOD_ROWS, tn), lambda l, j: (l, 0, j)),
        out_shape=jax.ShapeDtypeStruct((depth, MOD_ROWS, n6), F32),
        compiler_params=_cparams(("arbitrary", "arbitrary")),
        name="mod",
    )(cc, ada_w, ada_b.reshape(depth, 1, n6))


def _qk_lane_perm():
    n = jnp.arange(LANES)
    half, m = n // 64, n % 64
    return (m // 32) * 64 + ((m % 32) // 16) * 32 + half * 16 + m % 16


def _swap_halves(x):
    return pltpu.roll(x, LANES // 2, 1)


def _inproj_kernel(*refs, rope):
    if rope:
        (x_ref, g_ref, sh_ref, sc_ref, w_ref, wga_ref, cq_ref, sq_ref, ck_ref, sk_ref,
         z_ref, ga_ref, h_scr) = refs
    else:
        x_ref, g_ref, sh_ref, sc_ref, w_ref, wga_ref, z_ref, ga_ref, h_scr = refs
    j = pl.program_id(1)

    @pl.when(j == 0)
    def _():
        h = _rms(x_ref[...], g_ref[...]) * (1.0 + sc_ref[...]) + sh_ref[...]
        hb = h.astype(BF16)
        h_scr[...] = hb
        ga_ref[...] = jnp.dot(hb, wga_ref[...], preferred_element_type=F32).astype(BF16)

    acc = jnp.dot(h_scr[...], w_ref[...], preferred_element_type=F32)

    @pl.when(j == 0)
    def _():
        qscale = DA_HEAD_DIM ** -0.5 * LOG2E
        if rope:
            cols = []
            for c in range(acc.shape[1] // LANES):
                xs = acc[:, c * LANES:(c + 1) * LANES]
                is_q = c < DA_WIDTH // LANES
                cos = cq_ref[...] if is_q else ck_ref[...]
                sin = sq_ref[...] if is_q else sk_ref[...]
                cols.append((xs * cos + _swap_halves(xs) * sin).astype(BF16))
            z_ref[...] = jnp.concatenate(cols, axis=1)
        else:
            z_ref[:, :DA_WIDTH] = (acc[:, :DA_WIDTH] * qscale).astype(BF16)
            z_ref[:, DA_WIDTH:] = acc[:, DA_WIDTH:].astype(BF16)

    @pl.when(j != 0)
    def _():
        z_ref[...] = acc.astype(BF16)


def _inproj_call(x2, g, mod3, w_main, w_ga, rope_tabs, rows_per_mod, mod_row0, tm):
    t, d = x2.shape
    tn = 1024
    rope = rope_tabs is not None
    mrow = lambda i, j: (mod_row0 + i // rows_per_mod, 0, 0)
    in_specs = [pl.BlockSpec((tm, d), lambda i, j: (i, 0)),
                pl.BlockSpec((1, d), lambda i, j: (0, 0)),
                pl.BlockSpec((None, 1, d), mrow),
                pl.BlockSpec((None, 1, d), lambda i, j: (mod_row0 + i // rows_per_mod, 0, 1)),
                pl.BlockSpec((d, tn), lambda i, j: (0, j)),
                pl.BlockSpec((d, LANES), lambda i, j: (0, 0))]
    args = [x2, g.reshape(1, d), mod3, mod3, w_main, w_ga]
    if rope:
        nseq = rope_tabs[0].shape[0] // tm
        for tab in rope_tabs:
            in_specs.append(pl.BlockSpec((tm, LANES), lambda i, j: (i % nseq, 0)))
            args.append(tab)
    return pl.pallas_call(
        functools.partial(_inproj_kernel, rope=rope),
        grid=(t // tm, Z_WIDTH // tn),
        in_specs=in_specs,
        out_specs=[pl.BlockSpec((tm, tn), lambda i, j: (i, j)),
                   pl.BlockSpec((tm, LANES), lambda i, j: (i, 0))],
        out_shape=[jax.ShapeDtypeStruct((t, Z_WIDTH), BF16),
                   jax.ShapeDtypeStruct((t, LANES), BF16)],
        scratch_shapes=[pltpu.VMEM((tm, d), BF16)],
        compiler_params=_cparams(("arbitrary", "arbitrary")),
        name="inproj_rope" if rope else "inproj",
    )(*args)


def _rope_tables(seq):
    t = jnp.arange(seq)
    nf = DA_HEAD_DIM // 4
    inv = ROPE_BASE ** (-jnp.arange(nf, dtype=F32) / nf)
    ang_r = (t // GRID_W).astype(F32)[:, None] * inv
    ang_c = (t % GRID_W).astype(F32)[:, None] * inv
    cos32 = jnp.concatenate([jnp.cos(ang_r), jnp.cos(ang_c)], axis=1)
    sin32 = jnp.concatenate([jnp.sin(ang_r), jnp.sin(ang_c)], axis=1)
    cos = jnp.tile(cos32, (1, LANES // 32))
    sin = jnp.concatenate([-sin32, -sin32, sin32, sin32], axis=1)
    qscale = DA_HEAD_DIM ** -0.5 * LOG2E
    return cos * qscale, sin * qscale, cos, sin


ATT_ROWS = 16


def _attn_kernel(*refs, n_src, lambda_init):
    lam_ref, subln_ref, q_ref = refs[:3]
    kv_refs = refs[3:3 + 2 * n_src]
    o_ref = refs[3 + 2 * n_src]
    tq = q_ref.shape[0]
    hw = 2 * DA_HEAD_DIM

    lp = lam_ref[...]
    lam = (jnp.exp(jnp.sum(lp[0:1] * lp[1:2], axis=-1, keepdims=True))
           - jnp.exp(jnp.sum(lp[2:3] * lp[3:4], axis=-1, keepdims=True)) + lambda_init)
    lane = lax.broadcasted_iota(jnp.int32, (tq, hw), 1)
    first = (lane % 64) < 32

    for h in range(DA_HEADS):
        qh = q_ref[:, h * hw:(h + 1) * hw]
        zero = jnp.zeros_like(qh)
        qq = jnp.concatenate([jnp.where(first, qh, zero), jnp.where(first, zero, qh)], axis=0)
        scores = []
        for s in range(n_src):
            kh = kv_refs[2 * s][:, h * hw:(h + 1) * hw]
            scores.append(lax.dot_general(qq, kh, (((1,), (1,)), ((), ())),
                                          preferred_element_type=F32))
        probs = [[] for _ in range(n_src)]
        recips = []
        for rb in range(0, 2 * tq, ATT_ROWS):
            parts = [sc[rb:rb + ATT_ROWS, :] for sc in scores]
            m = parts[0].max(axis=-1, keepdims=True)
            for part in parts[1:]:
                m = jnp.maximum(m, part.max(axis=-1, keepdims=True))
            denom = None
            for s in range(n_src):
                p = jnp.exp2(parts[s] - m)
                ps = p.sum(axis=-1, keepdims=True)
                denom = ps if denom is None else denom + ps
                probs[s].append(p.astype(BF16))
            recips.append(1.0 / denom)
        r = jnp.concatenate(recips, axis=0)
        o0 = jnp.zeros((tq, hw), F32)
        o1 = jnp.zeros((tq, hw), F32)
        for s in range(n_src):
            pb = jnp.concatenate(probs[s], axis=0)
            vh = kv_refs[2 * s + 1][:, h * hw:(h + 1) * hw]
            o0 = o0 + jnp.dot(pb[:tq], vh, preferred_element_type=F32)
            o1 = o1 + jnp.dot(pb[tq:], vh, preferred_element_type=F32)
        o = o0 * r[:tq] - (lam * r[tq:]) * o1
        o = _rms(o, subln_ref[...]) * (1.0 - lambda_init)
        o_ref[:, h * hw:(h + 1) * hw] = o.astype(BF16)


def _attn_call(da_lambda, subln, zq3, kv_srcs, lambda_init, tq):
    b, lq, _ = zq3.shape
    in_specs = [pl.BlockSpec((4, DA_HEAD_DIM), lambda bi, i: (0, 0)),
                pl.BlockSpec((1, 2 * DA_HEAD_DIM), lambda bi, i: (0, 0)),
                pl.BlockSpec((None, tq, DA_WIDTH), lambda bi, i: (bi, i, ZB_Q))]
    args = [da_lambda, subln.reshape(1, -1), zq3]
    for src in kv_srcs:
        lk = src.shape[1]
        in_specs.append(pl.BlockSpec((None, lk, DA_WIDTH), lambda bi, i: (bi, 0, ZB_K)))
        in_specs.append(pl.BlockSpec((None, lk, DA_WIDTH), lambda bi, i: (bi, 0, ZB_V)))
        args += [src, src]
    return pl.pallas_call(
        functools.partial(_attn_kernel, n_src=len(kv_srcs), lambda_init=lambda_init),
        grid=(b, lq // tq),
        in_specs=in_specs,
        out_specs=pl.BlockSpec((None, tq, DA_WIDTH), lambda bi, i: (bi, i, 0)),
        out_shape=jax.ShapeDtypeStruct((b, lq, DA_WIDTH), BF16),
        compiler_params=_cparams(("arbitrary", "arbitrary")),
        name="attn",
    )(*args)


CV_HALO = 16
CV_ROWS = 32


def _conv_kernel(prev_ref, main_ref, next_ref, dw_ref, dwb_ref, lng_ref, lnb_ref, o_ref, hs_scr, sh_scr):
    i = pl.program_id(1)
    n = pl.num_programs(1)
    tm = main_ref.shape[0]

    def glu(u):
        a = u[:, :CV_WIDTH].astype(F32)
        g = u[:, CV_WIDTH:].astype(F32)
        return a * _sigmoid(g)

    hs_scr[0:CV_HALO, :] = jnp.where(i > 0, glu(prev_ref[...]), 0.0)
    hs_scr[CV_HALO:CV_HALO + tm, :] = glu(main_ref[...])
    hs_scr[CV_HALO + tm:2 * CV_HALO + tm, :] = jnp.where(i < n - 1, glu(next_ref[...]), 0.0)

    span = tm + 2 * CV_HALO - SUBLANES
    for s in range(SUBLANES):
        sh_scr[s, 0:span, :] = hs_scr[s:s + span, :]

    off = CV_HALO - CV_KERNEL // 2
    for r in range(0, tm, CV_ROWS):
        acc = jnp.zeros((CV_ROWS, CV_WIDTH), F32) + dwb_ref[...]
        for j in range(CV_KERNEL):
            s, a = (off + j) % SUBLANES, (off + j) // SUBLANES * SUBLANES
            acc = acc + sh_scr[s, r + a:r + a + CV_ROWS, :] * dw_ref[j:j + 1, :]
        mu = jnp.mean(acc, axis=-1, keepdims=True)
        xc = acc - mu
        var = jnp.mean(xc * xc, axis=-1, keepdims=True)
        y = xc * lax.rsqrt(var + EPS) * lng_ref[...] + lnb_ref[...]
        o_ref[r:r + CV_ROWS, :] = (y * _sigmoid(y)).astype(BF16)


def _conv_call(z3, dw, dwb, lng, lnb, tm):
    b, l, _ = z3.shape
    nh = tm // CV_HALO
    last = l // CV_HALO - 1
    cw = 2 * CV_WIDTH
    vec = lambda v: v.reshape(1, CV_WIDTH)
    return pl.pallas_call(
        _conv_kernel,
        grid=(b, l // tm),
        in_specs=[pl.BlockSpec((None, CV_HALO, cw), lambda bi, i: (bi, jnp.maximum(i * nh - 1, 0), ZB_CIN)),
                  pl.BlockSpec((None, tm, cw), lambda bi, i: (bi, i, ZB_CIN)),
                  pl.BlockSpec((None, CV_HALO, cw), lambda bi, i: (bi, jnp.minimum((i + 1) * nh, last), ZB_CIN)),
                  pl.BlockSpec((CV_KERNEL, CV_WIDTH), lambda bi, i: (0, 0)),
                  pl.BlockSpec((1, CV_WIDTH), lambda bi, i: (0, 0)),
                  pl.BlockSpec((1, CV_WIDTH), lambda bi, i: (0, 0)),
                  pl.BlockSpec((1, CV_WIDTH), lambda bi, i: (0, 0))],
        out_specs=pl.BlockSpec((None, tm, CV_WIDTH), lambda bi, i: (bi, i, 0)),
        out_shape=jax.ShapeDtypeStruct((b, l, CV_WIDTH), BF16),
        scratch_shapes=[pltpu.VMEM((tm + 2 * CV_HALO, CV_WIDTH), F32),
                        pltpu.VMEM((SUBLANES, tm + 2 * CV_HALO, CV_WIDTH), F32)],
        compiler_params=_cparams(("arbitrary", "arbitrary")),
        name="conv",
    )(z3, z3, z3, dw, vec(dwb), vec(lng), vec(lnb))


def _exact_dot01(t01, x):
    hi = x.astype(BF16)
    r1 = x - hi.astype(F32)
    mid = r1.astype(BF16)
    lo = (r1 - mid.astype(F32)).astype(BF16)
    dot = lambda v: jnp.dot(t01, v, preferred_element_type=F32)
    return dot(hi) + dot(mid) + dot(lo)


def _gla_kernel(q_ref, k_ref, v_ref, gg_ref, ga_ref, aup_ref, ab_ref, gn_ref, s0_ref,
                y_ref, sfin_ref, o_scr, qd_scr, ds_scr, dt_scr, sin_scr, st_scr):
    seq = q_ref.shape[0]
    ck = GLA_CHUNK
    n_chunks = seq // ck
    cg = min(GLA_GROUP, n_chunks)
    rows = cg * ck
    n_groups = n_chunks // cg
    nh = GLA_HEADS

    ri = lax.broadcasted_iota(jnp.int32, (rows, rows), 0)
    ci = lax.broadcasted_iota(jnp.int32, (rows, rows), 1)
    same_chunk = (ri // ck) == (ci // ck)
    keep = (same_chunk & (ri >= ci), same_chunk & (ri <= ci))
    keep_bf = tuple(jnp.where(m, 1.0, 0.0).astype(BF16) for m in keep)
    lane_head = lax.broadcasted_iota(jnp.int32, (rows, GLA_KW), 1) // GLA_DK
    hrow = lax.broadcasted_iota(jnp.int32, (nh * ck, GLA_KW), 0) // ck
    hlane = lax.broadcasted_iota(jnp.int32, (nh * ck, GLA_KW), 1) // GLA_DK
    head_mask = hrow == hlane

    def phase_a(g, carry):
        r0 = pl.multiple_of(g * rows, rows)
        x = jnp.dot(ga_ref[pl.ds(r0, rows), :], aup_ref[...], preferred_element_type=F32) + ab_ref[...]
        la2 = (jnp.minimum(x, 0.0) - jnp.log(1.0 + jnp.exp(-jnp.abs(x)))) * (1.0 / GLA_NORMALIZER)
        q = q_ref[pl.ds(r0, rows), :].astype(F32) * (GLA_DK ** -0.5)
        k = k_ref[pl.ds(r0, rows), :].astype(F32)
        v = v_ref[pl.ds(r0, rows), :]
        o_parts = [jnp.zeros((rows, GLA_DV), F32) for _ in range(nh)]
        for d in range(2):
            la = la2[:, d * GLA_KW:(d + 1) * GLA_KW]
            b = _exact_dot01(keep_bf[d], la)
            last = ck - 1 if d == 0 else 0
            b_last = jnp.concatenate(
                [jnp.broadcast_to(b[c * ck + last:c * ck + last + 1, :], (ck, GLA_KW)) for c in range(cg)], axis=0)
            q_dec = (q * jnp.exp(b)).astype(BF16)
            k_inv = (k * jnp.exp(-b)).astype(BF16)
            kt_t = (k * jnp.exp(b_last - b)).T.astype(BF16)
            qd_scr[d, pl.ds(r0, rows), :] = q_dec
            dt_scr[d, g] = jnp.exp(b_last).T
            for h in range(nh):
                qm = jnp.where(lane_head == h, q_dec, jnp.zeros_like(q_dec))
                sc = lax.dot_general(qm, k_inv, (((1,), (1,)), ((), ())), preferred_element_type=F32)
                sc = jnp.where(keep[d], sc, 0.0).astype(BF16)
                vh = v[:, h * GLA_DV:(h + 1) * GLA_DV]
                o_parts[h] = o_parts[h] + jnp.dot(sc, vh, preferred_element_type=F32)
                kth = jnp.concatenate([kt_t[h * GLA_DK:(h + 1) * GLA_DK, :]] * cg, axis=0)
                lhs = jnp.where(same_chunk, kth, jnp.zeros_like(kth))
                ds = jnp.dot(lhs, vh, preferred_element_type=F32)
                for c in range(cg):
                    ds_scr[d, g * cg + c, h * GLA_DK:(h + 1) * GLA_DK, :] = ds[c * ck:(c + 1) * ck, :]
        o_scr[pl.ds(r0, rows), :] = jnp.concatenate(o_parts, axis=1)
        return carry

    lax.fori_loop(0, n_groups, phase_a, 0, unroll=min(2, n_groups))

    st_scr[...] = s0_ref[...]

    def phase_b(i, carry):
        for d in range(2):
            g = i if d == 0 else n_groups - 1 - i
            dt = dt_scr[d, g]
            for j in range(cg):
                c = j if d == 0 else cg - 1 - j
                state = st_scr[d]
                sin_scr[d, g * cg + c] = state.astype(BF16)
                st_scr[d] = dt[:, c * ck:c * ck + 1] * state + ds_scr[d, g * cg + c]
        return carry

    lax.fori_loop(0, n_groups, phase_b, 0)
    sfin_ref[...] = st_scr[...]

    def phase_c(g, carry):
        r0 = pl.multiple_of(g * rows, rows)
        o = o_scr[pl.ds(r0, rows), :]
        inter = []
        for c in range(cg):
            rc = pl.multiple_of(r0 + c * ck, ck)
            acc = jnp.zeros((nh * ck, GLA_DV), F32)
            for d in range(2):
                qd = qd_scr[d, pl.ds(rc, ck), :]
                qm = jnp.where(head_mask, jnp.concatenate([qd] * nh, axis=0), jnp.zeros((nh * ck, GLA_KW), BF16))
                acc = acc + jnp.dot(qm, sin_scr[d, g * cg + c], preferred_element_type=F32)
            inter.append(jnp.concatenate([acc[h * ck:(h + 1) * ck, :] for h in range(nh)], axis=1))
        o = o + jnp.concatenate(inter, axis=0)
        gg = gg_ref[pl.ds(r0, rows), :].astype(F32)
        for h in range(nh):
            sl = slice(h * GLA_DV, (h + 1) * GLA_DV)
            gh = gg[:, sl]
            y_ref[pl.ds(r0, rows), sl] = (_rms(o[:, sl], gn_ref[...]) * (gh * _sigmoid(gh))).astype(BF16)
        return carry

    lax.fori_loop(0, n_groups, phase_c, 0, unroll=min(2, n_groups))


def _gla_call(z3, ga3, aup, ab, gnorm, s0):
    b, l, _ = z3.shape
    st_shape = (2, GLA_HEADS * GLA_DK, GLA_DV)
    n_chunks = l // GLA_CHUNK
    cg = min(GLA_GROUP, n_chunks)
    return pl.pallas_call(
        _gla_kernel,
        grid=(b,),
        in_specs=[pl.BlockSpec((None, l, GLA_KW), lambda bi: (bi, 0, ZB_GQ)),
                  pl.BlockSpec((None, l, GLA_KW), lambda bi: (bi, 0, ZB_GK)),
                  pl.BlockSpec((None, l, GLA_VW), lambda bi: (bi, 0, ZB_GV)),
                  pl.BlockSpec((None, l, GLA_VW), lambda bi: (bi, 0, ZB_GG)),
                  pl.BlockSpec((None, l, LANES), lambda bi: (bi, 0, 0)),
                  pl.BlockSpec((LANES, 2 * GLA_KW), lambda bi: (0, 0)),
                  pl.BlockSpec((1, 2 * GLA_KW), lambda bi: (0, 0)),
                  pl.BlockSpec((1, GLA_DV), lambda bi: (0, 0)),
                  pl.BlockSpec((None,) + st_shape, lambda bi: (bi, 0, 0, 0))],
        out_specs=[pl.BlockSpec((None, l, GLA_VW), lambda bi: (bi, 0, 0)),
                   pl.BlockSpec((None,) + st_shape, lambda bi: (bi, 0, 0, 0))],
        out_shape=[jax.ShapeDtypeStruct((b, l, GLA_VW), BF16),
                   jax.ShapeDtypeStruct((b,) + st_shape, F32)],
        scratch_shapes=[pltpu.VMEM((l, GLA_VW), F32),
                        pltpu.VMEM((2, l, GLA_KW), BF16),
                        pltpu.VMEM((2, n_chunks) + st_shape[1:], F32),
                        pltpu.VMEM((2, n_chunks // cg, st_shape[1], cg * GLA_CHUNK), F32),
                        pltpu.VMEM((2, n_chunks) + st_shape[1:], BF16),
                        pltpu.VMEM(st_shape, F32)],
        compiler_params=_cparams(("arbitrary",)),
        name="gla",
    )(z3, z3, z3, z3, ga3, aup, ab, gnorm.reshape(1, GLA_DV), s0)


def _merge_kernel(ya_ref, yb_ref, yc_ref, g0_ref, g1_ref, g2_ref, x_ref, gate_ref, sh_ref, sc_ref,
                  gpost_ref, gpre_ref, wa_ref, wb_ref, wc_ref, wo_ref, xo_ref, h_ref):
    ya = jnp.dot(ya_ref[...], wa_ref[...], preferred_element_type=F32)
    yb = jnp.dot(yb_ref[...], wb_ref[...], preferred_element_type=F32)
    yc = jnp.dot(yc_ref[...], wc_ref[...], preferred_element_type=F32)
    mix = (_sigmoid(g0_ref[...].astype(F32)) * ya + _sigmoid(g1_ref[...].astype(F32)) * yb
           + _sigmoid(g2_ref[...].astype(F32)) * yc)
    y = jnp.dot(mix.astype(BF16), wo_ref[...], preferred_element_type=F32)
    xn = x_ref[...] + gate_ref[...] * _rms(y, gpost_ref[...])
    xo_ref[...] = xn
    h_ref[...] = (_rms(xn, gpre_ref[...]) * (1.0 + sc_ref[...]) + sh_ref[...]).astype(BF16)


def _merge_call(ya, yb, yc, z, x2, mod3, g_post, g_pre, wa, wb, wc, wo, rows_per_mod, mod_row0, tm):
    t, d = x2.shape
    modspec = lambda k: pl.BlockSpec((None, 1, d), lambda i: (mod_row0 + i // rows_per_mod, 0, k))
    full = lambda a: pl.BlockSpec(a.shape, lambda i: (0,) * a.ndim)
    row = lambda w: pl.BlockSpec((tm, w), lambda i: (i, 0))
    in_specs = [row(DA_WIDTH), row(CV_WIDTH), row(GLA_VW),
                pl.BlockSpec((tm, d), lambda i: (i, ZB_GATES)),
                pl.BlockSpec((tm, d), lambda i: (i, ZB_GATES + 1)),
                pl.BlockSpec((tm, d), lambda i: (i, ZB_GATES + 2)),
                row(d), modspec(2), modspec(3), modspec(4),
                pl.BlockSpec((1, d), lambda i: (0, 0)), pl.BlockSpec((1, d), lambda i: (0, 0)),
                full(wa), full(wb), full(wc), full(wo)]
    return pl.pallas_call(
        _merge_kernel,
        grid=(t // tm,),
        in_specs=in_specs,
        out_specs=[row(d), row(d)],
        out_shape=[jax.ShapeDtypeStruct((t, d), F32), jax.ShapeDtypeStruct((t, d), BF16)],
        compiler_params=_cparams(("arbitrary",)),
        name="merge",
    )(ya, yb, yc, z, z, z, x2, mod3, mod3, mod3, g_post.reshape(1, d), g_pre.reshape(1, d), wa, wb, wc, wo)


FF_CHUNK = MXU_COLS


def _swiglu(h, w1_ref, w3_ref, w2_ref):
    acc = jnp.zeros((h.shape[0], w2_ref.shape[1]), F32)
    for c in range(0, w1_ref.shape[1], FF_CHUNK):
        u = jnp.dot(h, w1_ref[:, c:c + FF_CHUNK], preferred_element_type=F32)
        g = jnp.dot(h, w3_ref[:, c:c + FF_CHUNK], preferred_element_type=F32)
        a = (u * _sigmoid(u) * g).astype(BF16)
        acc = acc + jnp.dot(a, w2_ref[c:c + FF_CHUNK, :], preferred_element_type=F32)
    return acc


def _ffn_kernel(h_ref, x_ref, gate_ref, gpost_ref, w1_ref, w3_ref, w2_ref, xo_ref):
    f = _swiglu(h_ref[...], w1_ref, w3_ref, w2_ref)
    xo_ref[...] = x_ref[...] + gate_ref[...] * _rms(f, gpost_ref[...])


def _ffn_call(h2, x2, mod3, g_post, w1, w3, w2, rows_per_mod, mod_row0, tm):
    t, d = x2.shape
    full = lambda a: pl.BlockSpec(a.shape, lambda i: (0,) * a.ndim)
    row = pl.BlockSpec((tm, d), lambda i: (i, 0))
    return pl.pallas_call(
        _ffn_kernel,
        grid=(t // tm,),
        in_specs=[row, row,
                  pl.BlockSpec((None, 1, d), lambda i: (mod_row0 + i // rows_per_mod, 0, 5)),
                  pl.BlockSpec((1, d), lambda i: (0, 0)), full(w1), full(w3), full(w2)],
        out_specs=row,
        out_shape=jax.ShapeDtypeStruct((t, d), F32),
        compiler_params=_cparams(("arbitrary",)),
        name="ffn",
    )(h2, x2, mod3, g_post.reshape(1, d), w1, w3, w2)


def _top2_gate(logits):
    lane = lax.broadcasted_iota(jnp.int32, logits.shape, 1)
    neg = jnp.float32(-jnp.inf)
    lg = jnp.where(lane < N_EXPERTS, logits, neg)
    m1 = lg.max(axis=-1, keepdims=True)
    i1 = jnp.where(lg == m1, lane, LANES).min(axis=-1, keepdims=True)
    lg2 = jnp.where(lane == i1, neg, lg)
    m2 = lg2.max(axis=-1, keepdims=True)
    i2 = jnp.where(lg2 == m2, lane, LANES).min(axis=-1, keepdims=True)
    e2 = jnp.exp(m2 - m1)
    w1 = 1.0 / (1.0 + e2)
    gate = jnp.where(lane == i1, w1, jnp.where(lane == i2, e2 * w1, 0.0))
    sel = jnp.where(lane == i1, 1.0, jnp.where(lane == i2, 1.0, 0.0))
    return gate, sel


MOE_SMALL = MXU_COLS
MOE_LARGE = MXU_COLS * 3 // 2


def _moe_kernel(h_ref, rt_ref, w1_ref, w3_ref, w2_ref, f_ref, gate_scr, rank_scr):
    e = pl.program_id(1)
    tb = h_ref.shape[0]

    @pl.when(e == 0)
    def _():
        logits = jnp.dot(h_ref[...], rt_ref[...], preferred_element_type=F32)
        gate, sel = _top2_gate(logits)
        gate_scr[...] = gate
        r = lax.broadcasted_iota(jnp.int32, (tb, tb), 0)
        c = lax.broadcasted_iota(jnp.int32, (tb, tb), 1)
        before = jnp.where(c < r, 1.0, 0.0).astype(BF16)
        rank = jnp.dot(before, sel.astype(BF16), preferred_element_type=F32)
        rank_scr[...] = jnp.where(sel > 0.5, rank, -1.0).T
        f_ref[...] = jnp.zeros_like(f_ref)

    lane = lax.broadcasted_iota(jnp.int32, gate_scr.shape, 1)
    gcol = jnp.sum(jnp.where(lane == e, gate_scr[...], 0.0), axis=-1, keepdims=True)
    rank_row = rank_scr[pl.ds(e, 1), :]
    count = jnp.sum(jnp.where(rank_row >= 0.0, 1.0, 0.0))

    def expert_pass(start, rows):
        slot = lax.broadcasted_iota(jnp.int32, (rows, tb), 0).astype(F32) + start
        onehot = jnp.where(rank_row == slot, 1.0, 0.0).astype(BF16)
        hs = jnp.dot(onehot, h_ref[...], preferred_element_type=F32).astype(BF16)
        y = _swiglu(hs, w1_ref, w3_ref, w2_ref).astype(BF16)
        back = lax.dot_general(onehot, y, (((0,), (0,)), ((), ())), preferred_element_type=F32)
        f_ref[...] += gcol * back

    if tb <= MOE_SMALL:
        pl.when(count > 0)(functools.partial(expert_pass, 0, tb))
        return
    pl.when((count > 0) & (count <= MOE_SMALL))(functools.partial(expert_pass, 0, MOE_SMALL))
    pl.when(count > MOE_SMALL)(functools.partial(expert_pass, 0, MOE_LARGE))
    left = jnp.maximum(count.astype(jnp.int32) - MOE_LARGE, 0)
    n_tail = (left + MOE_SMALL - 1) // MOE_SMALL

    def tail(i, carry):
        expert_pass((MOE_LARGE + i * MOE_SMALL).astype(F32), MOE_SMALL)
        return carry

    lax.fori_loop(0, n_tail, tail, 0)


def _moe_call(h2, router_p, w1, w3, w2, tb):
    t, d = h2.shape
    ne, _, ff = w1.shape
    row = pl.BlockSpec((tb, d), lambda i, e: (i, 0))
    return pl.pallas_call(
        _moe_kernel,
        grid=(t // tb, ne),
        in_specs=[row,
                  pl.BlockSpec((d, LANES), lambda i, e: (0, 0)),
                  pl.BlockSpec((None, d, ff), lambda i, e: (e, 0, 0)),
                  pl.BlockSpec((None, d, ff), lambda i, e: (e, 0, 0)),
                  pl.BlockSpec((None, ff, d), lambda i, e: (e, 0, 0))],
        out_specs=row,
        out_shape=jax.ShapeDtypeStruct((t, d), F32),
        scratch_shapes=[pltpu.VMEM((tb, LANES), F32), pltpu.VMEM((LANES, tb), F32)],
        compiler_params=_cparams(("arbitrary", "arbitrary")),
        name="moe",
    )(h2, router_p, w1, w3, w2)


def _resid_kernel(f_ref, x_ref, gate_ref, gpost_ref, xo_ref):
    xo_ref[...] = x_ref[...] + gate_ref[...] * _rms(f_ref[...], gpost_ref[...])


def _resid_call(f, x2, mod3, g_post, rows_per_mod, mod_row0, tm):
    t, d = x2.shape
    row = pl.BlockSpec((tm, d), lambda i: (i, 0))
    return pl.pallas_call(
        _resid_kernel,
        grid=(t // tm,),
        in_specs=[row, row,
                  pl.BlockSpec((None, 1, d), lambda i: (mod_row0 + i // rows_per_mod, 0, 5)),
                  pl.BlockSpec((1, d), lambda i: (0, 0))],
        out_specs=row,
        out_shape=jax.ShapeDtypeStruct((t, d), F32),
        compiler_params=_cparams(("arbitrary",)),
        name="resid",
    )(f, x2, mod3, g_post.reshape(1, d))


MOE_TB = 1024
MOE_WIN = MXU_COLS
MOE_TG = 512
HALF = D_MODEL // 2


def _pack_words(rows):
    lo = lax.shift_right_logical(lax.bitcast_convert_type(rows[:, :HALF], jnp.uint32), jnp.uint32(16))
    hi = lax.bitcast_convert_type(rows[:, HALF:], jnp.uint32) & jnp.uint32(0xFFFF0000)
    return lo | hi


def _unpack_words(words):
    lo = lax.bitcast_convert_type(lax.shift_left(words, jnp.uint32(16)), F32)
    hi = lax.bitcast_convert_type(words & jnp.uint32(0xFFFF0000), F32)
    return jnp.concatenate([lo, hi], axis=1).astype(BF16)


def _route_kernel(h_ref, rt_ref, gate_ref, rank_ref, cnt_ref):
    tb = h_ref.shape[0]
    logits = jnp.dot(h_ref[...], rt_ref[...], preferred_element_type=F32)
    gate, sel = _top2_gate(logits)
    gate_ref[...] = gate
    r = lax.broadcasted_iota(jnp.int32, (tb, tb), 0)
    c = lax.broadcasted_iota(jnp.int32, (tb, tb), 1)
    before = jnp.where(c < r, 1.0, 0.0).astype(BF16)
    rank = jnp.dot(before, sel.astype(BF16), preferred_element_type=F32)
    rank_ref[...] = jnp.where(sel > 0.5, rank, -1.0).T
    cnt_ref[...] = jnp.sum(sel, axis=0, keepdims=True)


def _route_call(h2, router_p, tb):
    t, d = h2.shape
    nb = t // tb
    return pl.pallas_call(
        _route_kernel,
        grid=(nb,),
        in_specs=[pl.BlockSpec((tb, d), lambda i: (i, 0)), pl.BlockSpec((d, LANES), lambda i: (0, 0))],
        out_specs=[pl.BlockSpec((tb, LANES), lambda i: (i, 0)),
                   pl.BlockSpec((None, LANES, tb), lambda i: (i, 0, 0)),
                   pl.BlockSpec((None, 1, LANES), lambda i: (i, 0, 0))],
        out_shape=[jax.ShapeDtypeStruct((t, LANES), F32),
                   jax.ShapeDtypeStruct((nb, LANES, tb), F32),
                   jax.ShapeDtypeStruct((nb, 1, LANES), F32)],
        compiler_params=_cparams(("arbitrary",)),
        name="route",
    )(h2, router_p)


def _group_windows(cnt_ref, b):
    return lambda e: (cnt_ref[b * N_EXPERTS + e] + (MOE_WIN - 1)) // MOE_WIN


def _onehot_rows(rank_row, w, tb):
    slot = lax.broadcasted_iota(jnp.int32, (MOE_WIN, tb), 0).astype(F32) + (w * MOE_WIN).astype(F32)
    return jnp.where(rank_row == slot, 1.0, 0.0).astype(BF16)


def _pack_kernel(start_ref, cnt_ref, h_ref, rank_ref, hs_in_ref, hs_ref, buf, sem):
    del hs_in_ref
    b = pl.program_id(0)
    tb = h_ref.shape[0]
    nw = tb // MOE_WIN
    windows = _group_windows(cnt_ref, b)

    def copy(e, w):
        row0 = pl.multiple_of(start_ref[b * N_EXPERTS + e] + w * MOE_WIN, SUBLANES)
        return pltpu.make_async_copy(buf.at[e * nw + w], hs_ref.at[pl.ds(row0, MOE_WIN)], sem.at[e * nw + w])

    def pack_group(e, carry):
        rank_row = rank_ref[pl.ds(e, 1), :]

        def pack_window(w, c):
            rows = jnp.dot(_onehot_rows(rank_row, w, tb), h_ref[...], preferred_element_type=F32)
            buf[e * nw + w] = _pack_words(rows)
            copy(e, w).start()
            return c

        return lax.fori_loop(0, windows(e), pack_window, carry)

    lax.fori_loop(0, N_EXPERTS, pack_group, 0)

    def wait_group(e, carry):
        def wait_window(w, c):
            copy(e, w).wait()
            return c

        return lax.fori_loop(0, windows(e), wait_window, carry)

    lax.fori_loop(0, N_EXPERTS, wait_group, 0)


def _pack_call(start, cnt, h2, rank_t, hs_zero, tb):
    t, d = h2.shape
    nw = tb // MOE_WIN
    return pl.pallas_call(
        _pack_kernel,
        grid_spec=pltpu.PrefetchScalarGridSpec(
            num_scalar_prefetch=2,
            grid=(t // tb,),
            in_specs=[pl.BlockSpec((tb, d), lambda i, s, c: (i, 0)),
                      pl.BlockSpec((None, LANES, tb), lambda i, s, c: (i, 0, 0)),
                      pl.BlockSpec(memory_space=pl.ANY)],
            out_specs=pl.BlockSpec(memory_space=pl.ANY),
            scratch_shapes=[pltpu.VMEM((N_EXPERTS * nw, MOE_WIN, HALF), jnp.uint32),
                            pltpu.SemaphoreType.DMA((N_EXPERTS * nw,))]),
        out_shape=jax.ShapeDtypeStruct(hs_zero.shape, jnp.uint32),
        input_output_aliases={4: 0},
        compiler_params=_cparams(("arbitrary",)),
        name="pack",
    )(start, cnt, h2, rank_t, hs_zero)


def _expert_kernel(te_ref, nu_ref, hs_ref, w1_ref, w3_ref, w2_ref, ys_ref):
    del te_ref
    i = pl.program_id(0)

    @pl.when(i < nu_ref[0])
    def _():
        y = _swiglu(_unpack_words(hs_ref[...]), w1_ref, w3_ref, w2_ref)
        ys_ref[...] = _pack_words(y.astype(BF16).astype(F32))

    @pl.when(i >= nu_ref[0])
    def _():
        ys_ref[...] = jnp.zeros_like(ys_ref)


def _expert_call(tile_expert, n_used, hs, w1, w3, w2):
    r, half = hs.shape
    _, d, ff = w1.shape
    row = pl.BlockSpec((MOE_TG, half), lambda i, te, nu: (i, 0))
    return pl.pallas_call(
        _expert_kernel,
        grid_spec=pltpu.PrefetchScalarGridSpec(
            num_scalar_prefetch=2,
            grid=(r // MOE_TG,),
            in_specs=[row,
                      pl.BlockSpec((None, d, ff), lambda i, te, nu: (te[i], 0, 0)),
                      pl.BlockSpec((None, d, ff), lambda i, te, nu: (te[i], 0, 0)),
                      pl.BlockSpec((None, ff, d), lambda i, te, nu: (te[i], 0, 0))],
            out_specs=row),
        out_shape=jax.ShapeDtypeStruct((r, half), jnp.uint32),
        compiler_params=_cparams(("arbitrary",)),
        name="experts",
    )(tile_expert, n_used, hs, w1, w3, w2)


def _combine_kernel(start_ref, cnt_ref, gate_ref, rank_ref, x_ref, mgate_ref, gpost_ref, ys_ref,
                    xo_ref, buf, acc_scr, sem):
    b = pl.program_id(0)
    tb = x_ref.shape[0]
    nw = tb // MOE_WIN
    windows = _group_windows(cnt_ref, b)

    def copy(e, w):
        row0 = pl.multiple_of(start_ref[b * N_EXPERTS + e] + w * MOE_WIN, SUBLANES)
        return pltpu.make_async_copy(ys_ref.at[pl.ds(row0, MOE_WIN)], buf.at[e * nw + w], sem.at[e * nw + w])

    def fetch_group(e, carry):
        def fetch_window(w, c):
            copy(e, w).start()
            return c

        return lax.fori_loop(0, windows(e), fetch_window, carry)

    lax.fori_loop(0, N_EXPERTS, fetch_group, 0)
    acc_scr[...] = jnp.zeros_like(acc_scr)
    lane = lax.broadcasted_iota(jnp.int32, gate_ref.shape, 1)

    def expand_group(e, carry):
        rank_row = rank_ref[pl.ds(e, 1), :]
        gcol = jnp.sum(jnp.where(lane == e, gate_ref[...], 0.0), axis=-1, keepdims=True)

        def expand_window(w, c):
            copy(e, w).wait()
            live = lax.broadcasted_iota(jnp.int32, (MOE_WIN, 1), 0) < cnt_ref[b * N_EXPERTS + e] - w * MOE_WIN
            y = _unpack_words(jnp.where(live, buf[e * nw + w], jnp.uint32(0)))
            back = lax.dot_general(_onehot_rows(rank_row, w, tb), y, (((0,), (0,)), ((), ())),
                                   preferred_element_type=F32)
            acc_scr[...] += gcol * back
            return c

        return lax.fori_loop(0, windows(e), expand_window, carry)

    lax.fori_loop(0, N_EXPERTS, expand_group, 0)
    xo_ref[...] = x_ref[...] + mgate_ref[...] * _rms(acc_scr[...], gpost_ref[...])


def _combine_call(start, cnt, gate, rank_t, x2, mod3, g_post, ys, rows_per_mod, tb):
    t, d = x2.shape
    nw = tb // MOE_WIN
    row = pl.BlockSpec((tb, d), lambda i, s, c: (i, 0))
    return pl.pallas_call(
        _combine_kernel,
        grid_spec=pltpu.PrefetchScalarGridSpec(
            num_scalar_prefetch=2,
            grid=(t // tb,),
            in_specs=[pl.BlockSpec((tb, LANES), lambda i, s, c: (i, 0)),
                      pl.BlockSpec((None, LANES, tb), lambda i, s, c: (i, 0, 0)),
                      row,
                      pl.BlockSpec((None, 1, d), lambda i, s, c: (i // rows_per_mod, 0, 5)),
                      pl.BlockSpec((1, d), lambda i, s, c: (0, 0)),
                      pl.BlockSpec(memory_space=pl.ANY)],
            out_specs=row,
            scratch_shapes=[pltpu.VMEM((N_EXPERTS * nw, MOE_WIN, HALF), jnp.uint32),
                            pltpu.VMEM((tb, d), F32),
                            pltpu.SemaphoreType.DMA((N_EXPERTS * nw,))]),
        out_shape=jax.ShapeDtypeStruct((t, d), F32),
        compiler_params=_cparams(("arbitrary",)),
        name="combine",
    )(start, cnt, gate, rank_t, x2, mod3, g_post.reshape(1, d), ys)


def _moe_sorted(h2, x2, mod3, g_post, router_p, w1, w3, w2, rows_per_mod, tb):
    t, d = h2.shape
    nb = t // tb
    gate, rank_t, cnt3 = _route_call(h2, router_p, tb)
    cnt = cnt3[:, 0, :N_EXPERTS].astype(jnp.int32)
    seg = (cnt + SUBLANES - 1) // SUBLANES * SUBLANES
    cap = (seg.sum(axis=0) + MOE_WIN + MOE_TG - 1) // MOE_TG * MOE_TG
    ends = jnp.cumsum(cap)
    start = ((ends - cap)[None, :] + jnp.cumsum(seg, axis=0) - seg).reshape(-1)
    r_max = (TOP_K * t + N_EXPERTS * (nb * SUBLANES + MOE_WIN + MOE_TG)) // MOE_TG * MOE_TG
    tile_row = jnp.arange(r_max // MOE_TG, dtype=jnp.int32) * MOE_TG
    tile_expert = jnp.minimum(jnp.sum(tile_row[:, None] >= ends[None, :], axis=1), N_EXPERTS - 1).astype(jnp.int32)
    n_used = (ends[-1:] // MOE_TG).astype(jnp.int32)
    hs = _pack_call(start, cnt.reshape(-1), h2, rank_t, jnp.zeros((r_max, HALF), jnp.uint32), tb)
    ys = _expert_call(tile_expert, n_used, hs, w1, w3, w2)
    return _combine_call(start, cnt.reshape(-1), gate, rank_t, x2, mod3, g_post, ys, rows_per_mod, tb)


def _prep_w_in(w):
    o_q, o_k, o_v, o_c, o_gq, o_gk, o_gv, o_gg, o_ga, o_gt = 0, 512, 1024, 1536, 2560, 2816, 3072, 3584, 4096, 4128
    perm = (jnp.arange(DA_WIDTH // LANES)[:, None] * LANES + _qk_lane_perm()[None, :]).reshape(-1)
    main = jnp.concatenate([w[:, o_q:o_k][:, perm], w[:, o_k:o_v][:, perm], w[:, o_v:o_c], w[:, o_gv:o_gg], w[:, o_c:o_gq],
                            w[:, o_gg:o_ga], w[:, o_gq:o_gk], w[:, o_gk:o_gv], w[:, o_gt:]], axis=1)
    ga = jnp.pad(w[:, o_ga:o_gt], ((0, 0), (0, LANES - 2 * GLA_RANK)))
    return main.astype(BF16), ga.astype(BF16)


def _prep_a_up(a_up, a_b):
    m = jnp.zeros((LANES, 2 * GLA_KW), F32)
    m = m.at[:GLA_RANK, :GLA_KW].set(a_up[0]).at[GLA_RANK:2 * GLA_RANK, GLA_KW:].set(a_up[1])
    return m.astype(BF16), a_b.reshape(1, 2 * GLA_KW).astype(F32)


def _row_tile(n, pref):
    t = min(n, pref)
    assert n % t == 0
    return t


def kernel(x, c, ctx, c_ctx, ada_w, ada_b, g_mix_pre, g_mix_post, g_ffn_pre, g_ffn_post, w_in, da_lambda, da_subln, da_proj, cv_dw, cv_dw_b, cv_ln_g, cv_ln_b, cv_proj, gla_a_up, gla_a_b, gla_norm, gla_proj, w_out, ffn_w1, ffn_w3, ffn_w2, moe_router, moe_w1, moe_w3, moe_w2):
    bsz, seq, d = x.shape
    lc = ctx.shape[1]
    depth = w_in.shape[0]
    assert bsz + 1 <= MOD_ROWS and d == D_MODEL
    t_lat, t_ctx = bsz * seq, bsz * lc

    cc = jnp.zeros((MOD_ROWS, d), F32).at[:bsz].set(c).at[bsz].set(c_ctx)
    mod_all = _mod_call(cc, ada_w, ada_b)
    rope_tabs = _rope_tables(seq)

    tm_lat = _row_tile(seq, 2048)
    tm_ctx = _row_tile(t_ctx, 1024)
    tm_mix = _row_tile(seq, 512)
    tm_mix_ctx = _row_tile(t_ctx, 512)
    tq = _row_tile(seq, 256)
    tq_ctx = _row_tile(lc, 128)
    tc = _row_tile(seq, 256)
    tc_ctx = _row_tile(lc, 256)

    xl = x.reshape(t_lat, d)
    xc = ctx.reshape(t_ctx, d)
    for l in range(depth):
        need_ctx = l < depth - 1
        lambda_init = 0.8 - 0.6 * math.exp(-0.3 * l)
        mod3 = mod_all[l].reshape(MOD_ROWS, 1, 6 * d)
        w_main, w_ga = _prep_w_in(w_in[l])
        aup, ab = _prep_a_up(gla_a_up[l], gla_a_b[l])
        wa, wb, wc, wo = (da_proj[l].astype(BF16), cv_proj[l].astype(BF16),
                          gla_proj[l].astype(BF16), w_out[l].astype(BF16))

        z, ga = _inproj_call(xl, g_mix_pre[l], mod3, w_main, w_ga, rope_tabs, seq // tm_lat, 0, tm_lat)
        zc, gac = _inproj_call(xc, g_mix_pre[l], mod3, w_main, w_ga, None, t_ctx, bsz, tm_ctx)
        z3, zc3 = z.reshape(bsz, seq, Z_WIDTH), zc.reshape(bsz, lc, Z_WIDTH)

        ya = _attn_call(da_lambda[l], da_subln[l], z3, [z3, zc3], lambda_init, tq)
        yb = _conv_call(z3, cv_dw[l], cv_dw_b[l], cv_ln_g[l], cv_ln_b[l], tc)
        s_zero = jnp.zeros((bsz, 2, GLA_HEADS * GLA_DK, GLA_DV), F32)
        ycc, s_ctx = _gla_call(zc3, gac.reshape(bsz, lc, LANES), aup, ab, gla_norm[l], s_zero)
        yc, _ = _gla_call(z3, ga.reshape(bsz, seq, LANES), aup, ab, gla_norm[l], s_ctx)

        xl, h2 = _merge_call(ya.reshape(t_lat, -1), yb.reshape(t_lat, -1), yc.reshape(t_lat, -1), z, xl, mod3,
                             g_mix_post[l], g_ffn_pre[l], wa, wb, wc, wo, seq // tm_mix, 0, tm_mix)
        if need_ctx:
            yac = _attn_call(da_lambda[l], da_subln[l], zc3, [zc3], lambda_init, tq_ctx)
            ybc = _conv_call(zc3, cv_dw[l], cv_dw_b[l], cv_ln_g[l], cv_ln_b[l], tc_ctx)
            xc, hc2 = _merge_call(yac.reshape(t_ctx, -1), ybc.reshape(t_ctx, -1), ycc.reshape(t_ctx, -1), zc, xc,
                                  mod3, g_mix_post[l], g_ffn_pre[l], wa, wb, wc, wo, t_ctx, bsz, tm_mix_ctx)

        i = l // 2
        if l % 2 == 0:
            w1, w3, w2 = ffn_w1[i].astype(BF16), ffn_w3[i].astype(BF16), ffn_w2[i].astype(BF16)
            xl = _ffn_call(h2, xl, mod3, g_ffn_post[l], w1, w3, w2, seq // tm_mix, 0, tm_mix)
            if need_ctx:
                xc = _ffn_call(hc2, xc, mod3, g_ffn_post[l], w1, w3, w2, t_ctx, bsz, tm_mix_ctx)
        else:
            router_p = jnp.pad(moe_router[i], ((0, 0), (0, LANES - N_EXPERTS))).astype(BF16)
            w1, w3, w2 = moe_w1[i].astype(BF16), moe_w3[i].astype(BF16), moe_w2[i].astype(BF16)
            if seq % MOE_TB == 0:
                xl = _moe_sorted(h2, xl, mod3, g_ffn_post[l], router_p, w1, w3, w2, seq // MOE_TB, MOE_TB)
            else:
                f = _moe_call(h2, router_p, w1, w3, w2, _row_tile(t_lat, 1024))
                xl = _resid_call(f, xl, mod3, g_ffn_post[l], seq // tm_mix, 0, tm_mix)
            if need_ctx:
                fc = _moe_call(hc2, router_p, w1, w3, w2, _row_tile(t_ctx, 1024))
                xc = _resid_call(fc, xc, mod3, g_ffn_post[l], t_ctx, bsz, tm_mix_ctx)
    return xl.reshape(bsz, seq, d)
```

```python
import functools
import math

import jax
import jax.numpy as jnp
from jax import lax
from jax.experimental import pallas as pl
from jax.experimental.pallas import tpu as pltpu

F32 = jnp.float32
BF16 = jnp.bfloat16
HIGHEST = lax.Precision.HIGHEST

D_MODEL = 1024
GRID_W = 64
DA_HEADS = 4
DA_HEAD_DIM = 64
DA_WIDTH = 512
ROPE_BASE = 10000.0
CV_WIDTH = 512
CV_KERNEL = 31
GLA_HEADS = 4
GLA_DK = 64
GLA_DV = 128
GLA_KW = 256
GLA_VW = 512
GLA_RANK = 16
GLA_NORMALIZER = 16.0
GLA_CHUNK = 64
GLA_GROUP = 4
N_EXPERTS = 8
TOP_K = 2
D_FF = 2816
EPS = 1e-6
LOG2E = 1.4426950408889634

V7X_VMEM_BYTES = 64 * 1024 * 1024
VMEM_LIMIT = V7X_VMEM_BYTES - 8 * 1024 * 1024
LANES = 128
SUBLANES = 8
MXU_COLS = 256

Z_WIDTH = 7168
ZB_Q, ZB_K, ZB_V, ZB_GV = 0, 1, 2, 3
ZB_CIN = 2
ZB_GG = 6
ZB_GQ, ZB_GK = 14, 15
ZB_GATES = 4
MOD_ROWS = 40


def _cparams(sem, vmem=VMEM_LIMIT):
    return pltpu.CompilerParams(dimension_semantics=sem, vmem_limit_bytes=vmem)


def _sigmoid(x):
    return 1.0 / (1.0 + jnp.exp(-x))


def _rms(x, g):
    return x * lax.rsqrt(jnp.mean(x * x, axis=-1, keepdims=True) + EPS) * g


def _mod_kernel(c_ref, w_ref, b_ref, o_ref):
    c = c_ref[...]
    s = c * _sigmoid(c)
    o_ref[...] = jnp.dot(s, w_ref[...], preferred_element_type=F32, precision=HIGHEST) + b_ref[...]


def _mod_call(cc, ada_w, ada_b):
    depth, d, n6 = ada_w.shape
    tn = 1536
    return pl.pallas_call(
        _mod_kernel,
        grid=(depth, n6 // tn),
        in_specs=[pl.BlockSpec((MOD_ROWS, d), lambda l, j: (0, 0)),
                  pl.BlockSpec((None, d, tn), lambda l, j: (l, 0, j)),
                  pl.BlockSpec((None, 1, tn), lambda l, j: (l, 0, j))],
        out_specs=pl.BlockSpec((None, MOD_ROWS, tn), lambda l, j: (l, 0, j)),
        out_shape=jax.ShapeDtypeStruct((depth, MOD_ROWS, n6), F32),
        compiler_params=_cparams(("arbitrary", "arbitrary")),
        name="mod",
    )(cc, ada_w, ada_b.reshape(depth, 1, n6))


def _qk_lane_perm():
    n = jnp.arange(LANES)
    half, m = n // 64, n % 64
    return (m // 32) * 64 + ((m % 32) // 16) * 32 + half * 16 + m % 16


def _swap_halves(x):
    return pltpu.roll(x, LANES // 2, 1)


def _inproj_kernel(*refs, rope):
    if rope:
        (x_ref, g_ref, sh_ref, sc_ref, w_ref, wga_ref, cq_ref, sq_ref, ck_ref, sk_ref,
         z_ref, ga_ref, h_scr) = refs
    else:
        x_ref, g_ref, sh_ref, sc_ref, w_ref, wga_ref, z_ref, ga_ref, h_scr = refs
    j = pl.program_id(1)

    @pl.when(j == 0)
    def _():
        h = _rms(x_ref[...], g_ref[...]) * (1.0 + sc_ref[...]) + sh_ref[...]
        hb = h.astype(BF16)
        h_scr[...] = hb
        ga_ref[...] = jnp.dot(hb, wga_ref[...], preferred_element_type=F32).astype(BF16)

    acc = jnp.dot(h_scr[...], w_ref[...], preferred_element_type=F32)

    @pl.when(j == 0)
    def _():
        qscale = DA_HEAD_DIM ** -0.5 * LOG2E
        if rope:
            cols = []
            for c in range(acc.shape[1] // LANES):
                xs = acc[:, c * LANES:(c + 1) * LANES]
                is_q = c < DA_WIDTH // LANES
                cos = cq_ref[...] if is_q else ck_ref[...]
                sin = sq_ref[...] if is_q else sk_ref[...]
                cols.append((xs * cos + _swap_halves(xs) * sin).astype(BF16))
            z_ref[...] = jnp.concatenate(cols, axis=1)
        else:
            z_ref[:, :DA_WIDTH] = (acc[:, :DA_WIDTH] * qscale).astype(BF16)
            z_ref[:, DA_WIDTH:] = acc[:, DA_WIDTH:].astype(BF16)

    @pl.when(j != 0)
    def _():
        z_ref[...] = acc.astype(BF16)


def _inproj_call(x2, g, mod3, w_main, w_ga, rope_tabs, rows_per_mod, mod_row0, tm):
    t, d = x2.shape
    tn = 1024
    rope = rope_tabs is not None
    mrow = lambda i, j: (mod_row0 + i // rows_per_mod, 0, 0)
    in_specs = [pl.BlockSpec((tm, d), lambda i, j: (i, 0)),
                pl.BlockSpec((1, d), lambda i, j: (0, 0)),
                pl.BlockSpec((None, 1, d), mrow),
                pl.BlockSpec((None, 1, d), lambda i, j: (mod_row0 + i // rows_per_mod, 0, 1)),
                pl.BlockSpec((d, tn), lambda i, j: (0, j)),
                pl.BlockSpec((d, LANES), lambda i, j: (0, 0))]
    args = [x2, g.reshape(1, d), mod3, mod3, w_main, w_ga]
    if rope:
        nseq = rope_tabs[0].shape[0] // tm
        for tab in rope_tabs:
            in_specs.append(pl.BlockSpec((tm, LANES), lambda i, j: (i % nseq, 0)))
            args.append(tab)
    return pl.pallas_call(
        functools.partial(_inproj_kernel, rope=rope),
        grid=(t // tm, Z_WIDTH // tn),
        in_specs=in_specs,
        out_specs=[pl.BlockSpec((tm, tn), lambda i, j: (i, j)),
                   pl.BlockSpec((tm, LANES), lambda i, j: (i, 0))],
        out_shape=[jax.ShapeDtypeStruct((t, Z_WIDTH), BF16),
                   jax.ShapeDtypeStruct((t, LANES), BF16)],
        scratch_shapes=[pltpu.VMEM((tm, d), BF16)],
        compiler_params=_cparams(("arbitrary", "arbitrary")),
        name="inproj_rope" if rope else "inproj",
    )(*args)


def _rope_tables(seq):
    t = jnp.arange(seq)
    nf = DA_HEAD_DIM // 4
    inv = ROPE_BASE ** (-jnp.arange(nf, dtype=F32) / nf)
    ang_r = (t // GRID_W).astype(F32)[:, None] * inv
    ang_c = (t % GRID_W).astype(F32)[:, None] * inv
    cos32 = jnp.concatenate([jnp.cos(ang_r), jnp.cos(ang_c)], axis=1)
    sin32 = jnp.concatenate([jnp.sin(ang_r), jnp.sin(ang_c)], axis=1)
    cos = jnp.tile(cos32, (1, LANES // 32))
    sin = jnp.concatenate([-sin32, -sin32, sin32, sin32], axis=1)
    qscale = DA_HEAD_DIM ** -0.5 * LOG2E
    return cos * qscale, sin * qscale, cos, sin


ATT_ROWS = 16


def _attn_kernel(*refs, n_src, lambda_init):
    lam_ref, subln_ref, q_ref = refs[:3]
    kv_refs = refs[3:3 + 2 * n_src]
    o_ref = refs[3 + 2 * n_src]
    tq = q_ref.shape[0]
    hw = 2 * DA_HEAD_DIM

    lp = lam_ref[...]
    lam = (jnp.exp(jnp.sum(lp[0:1] * lp[1:2], axis=-1, keepdims=True))
           - jnp.exp(jnp.sum(lp[2:3] * lp[3:4], axis=-1, keepdims=True)) + lambda_init)
    lane = lax.broadcasted_iota(jnp.int32, (tq, hw), 1)
    first = (lane % 64) < 32

    for h in range(DA_HEADS):
        qh = q_ref[:, h * hw:(h + 1) * hw]
        zero = jnp.zeros_like(qh)
        qq = jnp.concatenate([jnp.where(first, qh, zero), jnp.where(first, zero, qh)], axis=0)
        scores = []
        for s in range(n_src):
            kh = kv_refs[2 * s][:, h * hw:(h + 1) * hw]
            scores.append(lax.dot_general(qq, kh, (((1,), (1,)), ((), ())),
                                          preferred_element_type=F32))
        probs = [[] for _ in range(n_src)]
        recips = []
        for rb in range(0, 2 * tq, ATT_ROWS):
            parts = [sc[rb:rb + ATT_ROWS, :] for sc in scores]
            m = parts[0].max(axis=-1, keepdims=True)
            for part in parts[1:]:
                m = jnp.maximum(m, part.max(axis=-1, keepdims=True))
            denom = None
            for s in range(n_src):
                p = jnp.exp2(parts[s] - m)
                ps = p.sum(axis=-1, keepdims=True)
                denom = ps if denom is None else denom + ps
                probs[s].append(p.astype(BF16))
            recips.append(1.0 / denom)
        r = jnp.concatenate(recips, axis=0)
        o0 = jnp.zeros((tq, hw), F32)
        o1 = jnp.zeros((tq, hw), F32)
        for s in range(n_src):
            pb = jnp.concatenate(probs[s], axis=0)
            vh = kv_refs[2 * s + 1][:, h * hw:(h + 1) * hw]
            o0 = o0 + jnp.dot(pb[:tq], vh, preferred_element_type=F32)
            o1 = o1 + jnp.dot(pb[tq:], vh, preferred_element_type=F32)
        o = o0 * r[:tq] - (lam * r[tq:]) * o1
        o = _rms(o, subln_ref[...]) * (1.0 - lambda_init)
        o_ref[:, h * hw:(h + 1) * hw] = o.astype(BF16)


def _attn_call(da_lambda, subln, zq3, kv_srcs, lambda_init, tq):
    b, lq, _ = zq3.shape
    in_specs = [pl.BlockSpec((4, DA_HEAD_DIM), lambda bi, i: (0, 0)),
                pl.BlockSpec((1, 2 * DA_HEAD_DIM), lambda bi, i: (0, 0)),
                pl.BlockSpec((None, tq, DA_WIDTH), lambda bi, i: (bi, i, ZB_Q))]
    args = [da_lambda, subln.reshape(1, -1), zq3]
    for src in kv_srcs:
        lk = src.shape[1]
        in_specs.append(pl.BlockSpec((None, lk, DA_WIDTH), lambda bi, i: (bi, 0, ZB_K)))
        in_specs.append(pl.BlockSpec((None, lk, DA_WIDTH), lambda bi, i: (bi, 0, ZB_V)))
        args += [src, src]
    return pl.pallas_call(
        functools.partial(_attn_kernel, n_src=len(kv_srcs), lambda_init=lambda_init),
        grid=(b, lq // tq),
        in_specs=in_specs,
        out_specs=pl.BlockSpec((None, tq, DA_WIDTH), lambda bi, i: (bi, i, 0)),
        out_shape=jax.ShapeDtypeStruct((b, lq, DA_WIDTH), BF16),
        compiler_params=_cparams(("arbitrary", "arbitrary")),
        name="attn",
    )(*args)


CV_HALO = 16
CV_ROWS = 32


def _conv_kernel(prev_ref, main_ref, next_ref, dw_ref, dwb_ref, lng_ref, lnb_ref, o_ref, hs_scr, sh_scr):
    i = pl.program_id(1)
    n = pl.num_programs(1)
    tm = main_ref.shape[0]

    def glu(u):
        a = u[:, :CV_WIDTH].astype(F32)
        g = u[:, CV_WIDTH:].astype(F32)
        return a * _sigmoid(g)

    hs_scr[0:CV_HALO, :] = jnp.where(i > 0, glu(prev_ref[...]), 0.0)
    hs_scr[CV_HALO:CV_HALO + tm, :] = glu(main_ref[...])
    hs_scr[CV_HALO + tm:2 * CV_HALO + tm, :] = jnp.where(i < n - 1, glu(next_ref[...]), 0.0)

    span = tm + 2 * CV_HALO - SUBLANES
    for s in range(SUBLANES):
        sh_scr[s, 0:span, :] = hs_scr[s:s + span, :]

    off = CV_HALO - CV_KERNEL // 2
    for r in range(0, tm, CV_ROWS):
        acc = jnp.zeros((CV_ROWS, CV_WIDTH), F32) + dwb_ref[...]
        for j in range(CV_KERNEL):
            s, a = (off + j) % SUBLANES, (off + j) // SUBLANES * SUBLANES
            acc = acc + sh_scr[s, r + a:r + a + CV_ROWS, :] * dw_ref[j:j + 1, :]
        mu = jnp.mean(acc, axis=-1, keepdims=True)
        xc = acc - mu
        var = jnp.mean(xc * xc, axis=-1, keepdims=True)
        y = xc * lax.rsqrt(var + EPS) * lng_ref[...] + lnb_ref[...]
        o_ref[r:r + CV_ROWS, :] = (y * _sigmoid(y)).astype(BF16)


def _conv_call(z3, dw, dwb, lng, lnb, tm):
    b, l, _ = z3.shape
    nh = tm // CV_HALO
    last = l // CV_HALO - 1
    cw = 2 * CV_WIDTH
    vec = lambda v: v.reshape(1, CV_WIDTH)
    return pl.pallas_call(
        _conv_kernel,
        grid=(b, l // tm),
        in_specs=[pl.BlockSpec((None, CV_HALO, cw), lambda bi, i: (bi, jnp.maximum(i * nh - 1, 0), ZB_CIN)),
                  pl.BlockSpec((None, tm, cw), lambda bi, i: (bi, i, ZB_CIN)),
                  pl.BlockSpec((None, CV_HALO, cw), lambda bi, i: (bi, jnp.minimum((i + 1) * nh, last), ZB_CIN)),
                  pl.BlockSpec((CV_KERNEL, CV_WIDTH), lambda bi, i: (0, 0)),
                  pl.BlockSpec((1, CV_WIDTH), lambda bi, i: (0, 0)),
                  pl.BlockSpec((1, CV_WIDTH), lambda bi, i: (0, 0)),
                  pl.BlockSpec((1, CV_WIDTH), lambda bi, i: (0, 0))],
        out_specs=pl.BlockSpec((None, tm, CV_WIDTH), lambda bi, i: (bi, i, 0)),
        out_shape=jax.ShapeDtypeStruct((b, l, CV_WIDTH), BF16),
        scratch_shapes=[pltpu.VMEM((tm + 2 * CV_HALO, CV_WIDTH), F32),
                        pltpu.VMEM((SUBLANES, tm + 2 * CV_HALO, CV_WIDTH), F32)],
        compiler_params=_cparams(("arbitrary", "arbitrary")),
        name="conv",
    )(z3, z3, z3, dw, vec(dwb), vec(lng), vec(lnb))


def _exact_dot01(t01, x):
    hi = x.astype(BF16)
    r1 = x - hi.astype(F32)
    mid = r1.astype(BF16)
    lo = (r1 - mid.astype(F32)).astype(BF16)
    dot = lambda v: jnp.dot(t01, v, preferred_element_type=F32)
    return dot(hi) + dot(mid) + dot(lo)


def _gla_kernel(q_ref, k_ref, v_ref, gg_ref, ga_ref, aup_ref, ab_ref, gn_ref, s0_ref,
                y_ref, sfin_ref, o_scr, qd_scr, ds_scr, dt_scr, sin_scr, st_scr):
    seq = q_ref.shape[0]
    ck = GLA_CHUNK
    n_chunks = seq // ck
    cg = min(GLA_GROUP, n_chunks)
    rows = cg * ck
    n_groups = n_chunks // cg
    nh = GLA_HEADS

    ri = lax.broadcasted_iota(jnp.int32, (rows, rows), 0)
    ci = lax.broadcasted_iota(jnp.int32, (rows, rows), 1)
    same_chunk = (ri // ck) == (ci // ck)
    keep = (same_chunk & (ri >= ci), same_chunk & (ri <= ci))
    keep_bf = tuple(jnp.where(m, 1.0, 0.0).astype(BF16) for m in keep)
    lane_head = lax.broadcasted_iota(jnp.int32, (rows, GLA_KW), 1) // GLA_DK
    hrow = lax.broadcasted_iota(jnp.int32, (nh * ck, GLA_KW), 0) // ck
    hlane = lax.broadcasted_iota(jnp.int32, (nh * ck, GLA_KW), 1) // GLA_DK
    head_mask = hrow == hlane

    def phase_a(g, carry):
        r0 = pl.multiple_of(g * rows, rows)
        x = jnp.dot(ga_ref[pl.ds(r0, rows), :], aup_ref[...], preferred_element_type=F32) + ab_ref[...]
        la2 = (jnp.minimum(x, 0.0) - jnp.log(1.0 + jnp.exp(-jnp.abs(x)))) * (1.0 / GLA_NORMALIZER)
        q = q_ref[pl.ds(r0, rows), :].astype(F32) * (GLA_DK ** -0.5)
        k = k_ref[pl.ds(r0, rows), :].astype(F32)
        v = v_ref[pl.ds(r0, rows), :]
        o_parts = [jnp.zeros((rows, GLA_DV), F32) for _ in range(nh)]
        for d in range(2):
            la = la2[:, d * GLA_KW:(d + 1) * GLA_KW]
            b = _exact_dot01(keep_bf[d], la)
            last = ck - 1 if d == 0 else 0
            b_last = jnp.concatenate(
                [jnp.broadcast_to(b[c * ck + last:c * ck + last + 1, :], (ck, GLA_KW)) for c in range(cg)], axis=0)
            q_dec = (q * jnp.exp(b)).astype(BF16)
            k_inv = (k * jnp.exp(-b)).astype(BF16)
            kt_t = (k * jnp.exp(b_last - b)).T.astype(BF16)
            qd_scr[d, pl.ds(r0, rows), :] = q_dec
            dt_scr[d, g] = jnp.exp(b_last).T
            for h in range(nh):
                qm = jnp.where(lane_head == h, q_dec, jnp.zeros_like(q_dec))
                sc = lax.dot_general(qm, k_inv, (((1,), (1,)), ((), ())), preferred_element_type=F32)
                sc = jnp.where(keep[d], sc, 0.0).astype(BF16)
                vh = v[:, h * GLA_DV:(h + 1) * GLA_DV]
                o_parts[h] = o_parts[h] + jnp.dot(sc, vh, preferred_element_type=F32)
                kth = jnp.concatenate([kt_t[h * GLA_DK:(h + 1) * GLA_DK, :]] * cg, axis=0)
                lhs = jnp.where(same_chunk, kth, jnp.zeros_like(kth))
                ds = jnp.dot(lhs, vh, preferred_element_type=F32)
                for c in range(cg):
                    ds_scr[d, g * cg + c, h * GLA_DK:(h + 1) * GLA_DK, :] = ds[c * ck:(c + 1) * ck, :]
        o_scr[pl.ds(r0, rows), :] = jnp.concatenate(o_parts, axis=1)
        return carry

    lax.fori_loop(0, n_groups, phase_a, 0, unroll=min(2, n_groups))

    st_scr[...] = s0_ref[...]

    def phase_b(i, carry):
        for d in range(2):
            g = i if d == 0 else n_groups - 1 - i
            dt = dt_scr[d, g]
            for j in range(cg):
                c = j if d == 0 else cg - 1 - j
                state = st_scr[d]
                sin_scr[d, g * cg + c] = state.astype(BF16)
                st_scr[d] = dt[:, c * ck:c * ck + 1] * state + ds_scr[d, g * cg + c]
        return carry

    lax.fori_loop(0, n_groups, phase_b, 0)
    sfin_ref[...] = st_scr[...]

    def phase_c(g, carry):
        r0 = pl.multiple_of(g * rows, rows)
        o = o_scr[pl.ds(r0, rows), :]
        inter = []
        for c in range(cg):
            rc = pl.multiple_of(r0 + c * ck, ck)
            acc = jnp.zeros((nh * ck, GLA_DV), F32)
            for d in range(2):
                qd = qd_scr[d, pl.ds(rc, ck), :]
                qm = jnp.where(head_mask, jnp.concatenate([qd] * nh, axis=0), jnp.zeros((nh * ck, GLA_KW), BF16))
                acc = acc + jnp.dot(qm, sin_scr[d, g * cg + c], preferred_element_type=F32)
            inter.append(jnp.concatenate([acc[h * ck:(h + 1) * ck, :] for h in range(nh)], axis=1))
        o = o + jnp.concatenate(inter, axis=0)
        gg = gg_ref[pl.ds(r0, rows), :].astype(F32)
        for h in range(nh):
            sl = slice(h * GLA_DV, (h + 1) * GLA_DV)
            gh = gg[:, sl]
            y_ref[pl.ds(r0, rows), sl] = (_rms(o[:, sl], gn_ref[...]) * (gh * _sigmoid(gh))).astype(BF16)
        return carry

    lax.fori_loop(0, n_groups, phase_c, 0, unroll=min(2, n_groups))


def _gla_call(z3, ga3, aup, ab, gnorm, s0):
    b, l, _ = z3.shape
    st_shape = (2, GLA_HEADS * GLA_DK, GLA_DV)
    n_chunks = l // GLA_CHUNK
    cg = min(GLA_GROUP, n_chunks)
    return pl.pallas_call(
        _gla_kernel,
        grid=(b,),
        in_specs=[pl.BlockSpec((None, l, GLA_KW), lambda bi: (bi, 0, ZB_GQ)),
                  pl.BlockSpec((None, l, GLA_KW), lambda bi: (bi, 0, ZB_GK)),
                  pl.BlockSpec((None, l, GLA_VW), lambda bi: (bi, 0, ZB_GV)),
                  pl.BlockSpec((None, l, GLA_VW), lambda bi: (bi, 0, ZB_GG)),
                  pl.BlockSpec((None, l, LANES), lambda bi: (bi, 0, 0)),
                  pl.BlockSpec((LANES, 2 * GLA_KW), lambda bi: (0, 0)),
                  pl.BlockSpec((1, 2 * GLA_KW), lambda bi: (0, 0)),
                  pl.BlockSpec((1, GLA_DV), lambda bi: (0, 0)),
                  pl.BlockSpec((None,) + st_shape, lambda bi: (bi, 0, 0, 0))],
        out_specs=[pl.BlockSpec((None, l, GLA_VW), lambda bi: (bi, 0, 0)),
                   pl.BlockSpec((None,) + st_shape, lambda bi: (bi, 0, 0, 0))],
        out_shape=[jax.ShapeDtypeStruct((b, l, GLA_VW), BF16),
                   jax.ShapeDtypeStruct((b,) + st_shape, F32)],
        scratch_shapes=[pltpu.VMEM((l, GLA_VW), F32),
                        pltpu.VMEM((2, l, GLA_KW), BF16),
                        pltpu.VMEM((2, n_chunks) + st_shape[1:], F32),
                        pltpu.VMEM((2, n_chunks // cg, st_shape[1], cg * GLA_CHUNK), F32),
                        pltpu.VMEM((2, n_chunks) + st_shape[1:], BF16),
                        pltpu.VMEM(st_shape, F32)],
        compiler_params=_cparams(("arbitrary",)),
        name="gla",
    )(z3, z3, z3, z3, ga3, aup, ab, gnorm.reshape(1, GLA_DV), s0)


def _merge_kernel(ya_ref, yb_ref, yc_ref, g0_ref, g1_ref, g2_ref, x_ref, gate_ref, sh_ref, sc_ref,
                  gpost_ref, gpre_ref, wa_ref, wb_ref, wc_ref, wo_ref, xo_ref, h_ref):
    ya = jnp.dot(ya_ref[...], wa_ref[...], preferred_element_type=F32)
    yb = jnp.dot(yb_ref[...], wb_ref[...], preferred_element_type=F32)
    yc = jnp.dot(yc_ref[...], wc_ref[...], preferred_element_type=F32)
    mix = (_sigmoid(g0_ref[...].astype(F32)) * ya + _sigmoid(g1_ref[...].astype(F32)) * yb
           + _sigmoid(g2_ref[...].astype(F32)) * yc)
    y = jnp.dot(mix.astype(BF16), wo_ref[...], preferred_element_type=F32)
    xn = x_ref[...] + gate_ref[...] * _rms(y, gpost_ref[...])
    xo_ref[...] = xn
    h_ref[...] = (_rms(xn, gpre_ref[...]) * (1.0 + sc_ref[...]) + sh_ref[...]).astype(BF16)


def _merge_call(ya, yb, yc, z, x2, mod3, g_post, g_pre, wa, wb, wc, wo, rows_per_mod, mod_row0, tm):
    t, d = x2.shape
    modspec = lambda k: pl.BlockSpec((None, 1, d), lambda i: (mod_row0 + i // rows_per_mod, 0, k))
    full = lambda a: pl.BlockSpec(a.shape, lambda i: (0,) * a.ndim)
    row = lambda w: pl.BlockSpec((tm, w), lambda i: (i, 0))
    in_specs = [row(DA_WIDTH), row(CV_WIDTH), row(GLA_VW),
                pl.BlockSpec((tm, d), lambda i: (i, ZB_GATES)),
                pl.BlockSpec((tm, d), lambda i: (i, ZB_GATES + 1)),
                pl.BlockSpec((tm, d), lambda i: (i, ZB_GATES + 2)),
                row(d), modspec(2), modspec(3), modspec(4),
                pl.BlockSpec((1, d), lambda i: (0, 0)), pl.BlockSpec((1, d), lambda i: (0, 0)),
                full(wa), full(wb), full(wc), full(wo)]
    return pl.pallas_call(
        _merge_kernel,
        grid=(t // tm,),
        in_specs=in_specs,
        out_specs=[row(d), row(d)],
        out_shape=[jax.ShapeDtypeStruct((t, d), F32), jax.ShapeDtypeStruct((t, d), BF16)],
        compiler_params=_cparams(("arbitrary",)),
        name="merge",
    )(ya, yb, yc, z, z, z, x2, mod3, mod3, mod3, g_post.reshape(1, d), g_pre.reshape(1, d), wa, wb, wc, wo)


FF_CHUNK = MXU_COLS


def _swiglu(h, w1_ref, w3_ref, w2_ref):
    acc = jnp.zeros((h.shape[0], w2_ref.shape[1]), F32)
    for c in range(0, w1_ref.shape[1], FF_CHUNK):
        u = jnp.dot(h, w1_ref[:, c:c + FF_CHUNK], preferred_element_type=F32)
        g = jnp.dot(h, w3_ref[:, c:c + FF_CHUNK], preferred_element_type=F32)
        a = (u * _sigmoid(u) * g).astype(BF16)
        acc = acc + jnp.dot(a, w2_ref[c:c + FF_CHUNK, :], preferred_element_type=F32)
    return acc


def _ffn_kernel(h_ref, x_ref, gate_ref, gpost_ref, w1_ref, w3_ref, w2_ref, xo_ref):
    f = _swiglu(h_ref[...], w1_ref, w3_ref, w2_ref)
    xo_ref[...] = x_ref[...] + gate_ref[...] * _rms(f, gpost_ref[...])


def _ffn_call(h2, x2, mod3, g_post, w1, w3, w2, rows_per_mod, mod_row0, tm):
    t, d = x2.shape
    full = lambda a: pl.BlockSpec(a.shape, lambda i: (0,) * a.ndim)
    row = pl.BlockSpec((tm, d), lambda i: (i, 0))
    return pl.pallas_call(
        _ffn_kernel,
        grid=(t // tm,),
        in_specs=[row, row,
                  pl.BlockSpec((None, 1, d), lambda i: (mod_row0 + i // rows_per_mod, 0, 5)),
                  pl.BlockSpec((1, d), lambda i: (0, 0)), full(w1), full(w3), full(w2)],
        out_specs=row,
        out_shape=jax.ShapeDtypeStruct((t, d), F32),
        compiler_params=_cparams(("arbitrary",)),
        name="ffn",
    )(h2, x2, mod3, g_post.reshape(1, d), w1, w3, w2)


def _top2_gate(logits):
    lane = lax.broadcasted_iota(jnp.int32, logits.shape, 1)
    neg = jnp.float32(-jnp.inf)
    lg = jnp.where(lane < N_EXPERTS, logits, neg)
    m1 = lg.max(axis=-1, keepdims=True)
    i1 = jnp.where(lg == m1, lane, LANES).min(axis=-1, keepdims=True)
    lg2 = jnp.where(lane == i1, neg, lg)
    m2 = lg2.max(axis=-1, keepdims=True)
    i2 = jnp.where(lg2 == m2, lane, LANES).min(axis=-1, keepdims=True)
    e2 = jnp.exp(m2 - m1)
    w1 = 1.0 / (1.0 + e2)
    gate = jnp.where(lane == i1, w1, jnp.where(lane == i2, e2 * w1, 0.0))
    sel = jnp.where(lane == i1, 1.0, jnp.where(lane == i2, 1.0, 0.0))
    return gate, sel


MOE_SMALL = MXU_COLS
MOE_LARGE = MXU_COLS * 3 // 2


def _moe_kernel(h_ref, rt_ref, w1_ref, w3_ref, w2_ref, f_ref, gate_scr, rank_scr):
    e = pl.program_id(1)
    tb = h_ref.shape[0]

    @pl.when(e == 0)
    def _():
        logits = jnp.dot(h_ref[...], rt_ref[...], preferred_element_type=F32)
        gate, sel = _top2_gate(logits)
        gate_scr[...] = gate
        r = lax.broadcasted_iota(jnp.int32, (tb, tb), 0)
        c = lax.broadcasted_iota(jnp.int32, (tb, tb), 1)
        before = jnp.where(c < r, 1.0, 0.0).astype(BF16)
        rank = jnp.dot(before, sel.astype(BF16), preferred_element_type=F32)
        rank_scr[...] = jnp.where(sel > 0.5, rank, -1.0).T
        f_ref[...] = jnp.zeros_like(f_ref)

    lane = lax.broadcasted_iota(jnp.int32, gate_scr.shape, 1)
    gcol = jnp.sum(jnp.where(lane == e, gate_scr[...], 0.0), axis=-1, keepdims=True)
    rank_row = rank_scr[pl.ds(e, 1), :]
    count = jnp.sum(jnp.where(rank_row >= 0.0, 1.0, 0.0))

    def expert_pass(start, rows):
        slot = lax.broadcasted_iota(jnp.int32, (rows, tb), 0).astype(F32) + start
        onehot = jnp.where(rank_row == slot, 1.0, 0.0).astype(BF16)
        hs = jnp.dot(onehot, h_ref[...], preferred_element_type=F32).astype(BF16)
        y = _swiglu(hs, w1_ref, w3_ref, w2_ref).astype(BF16)
        back = lax.dot_general(onehot, y, (((0,), (0,)), ((), ())), preferred_element_type=F32)
        f_ref[...] += gcol * back

    if tb <= MOE_SMALL:
        pl.when(count > 0)(functools.partial(expert_pass, 0, tb))
        return
    pl.when((count > 0) & (count <= MOE_SMALL))(functools.partial(expert_pass, 0, MOE_SMALL))
    pl.when(count > MOE_SMALL)(functools.partial(expert_pass, 0, MOE_LARGE))
    left = jnp.maximum(count.astype(jnp.int32) - MOE_LARGE, 0)
    n_tail = (left + MOE_SMALL - 1) // MOE_SMALL

    def tail(i, carry):
        expert_pass((MOE_LARGE + i * MOE_SMALL).astype(F32), MOE_SMALL)
        return carry

    lax.fori_loop(0, n_tail, tail, 0)


def _moe_call(h2, router_p, w1, w3, w2, tb):
    t, d = h2.shape
    ne, _, ff = w1.shape
    row = pl.BlockSpec((tb, d), lambda i, e: (i, 0))
    return pl.pallas_call(
        _moe_kernel,
        grid=(t // tb, ne),
        in_specs=[row,
                  pl.BlockSpec((d, LANES), lambda i, e: (0, 0)),
                  pl.BlockSpec((None, d, ff), lambda i, e: (e, 0, 0)),
                  pl.BlockSpec((None, d, ff), lambda i, e: (e, 0, 0)),
                  pl.BlockSpec((None, ff, d), lambda i, e: (e, 0, 0))],
        out_specs=row,
        out_shape=jax.ShapeDtypeStruct((t, d), F32),
        scratch_shapes=[pltpu.VMEM((tb, LANES), F32), pltpu.VMEM((LANES, tb), F32)],
        compiler_params=_cparams(("arbitrary", "arbitrary")),
        name="moe",
    )(h2, router_p, w1, w3, w2)


def _resid_kernel(f_ref, x_ref, gate_ref, gpost_ref, xo_ref):
    xo_ref[...] = x_ref[...] + gate_ref[...] * _rms(f_ref[...], gpost_ref[...])


def _resid_call(f, x2, mod3, g_post, rows_per_mod, mod_row0, tm):
    t, d = x2.shape
    row = pl.BlockSpec((tm, d), lambda i: (i, 0))
    return pl.pallas_call(
        _resid_kernel,
        grid=(t // tm,),
        in_specs=[row, row,
                  pl.BlockSpec((None, 1, d), lambda i: (mod_row0 + i // rows_per_mod, 0, 5)),
                  pl.BlockSpec((1, d), lambda i: (0, 0))],
        out_specs=row,
        out_shape=jax.ShapeDtypeStruct((t, d), F32),
        compiler_params=_cparams(("arbitrary",)),
        name="resid",
    )(f, x2, mod3, g_post.reshape(1, d))


MOE_TB = 1024
MOE_WIN = MXU_COLS
MOE_TG = 512
HALF = D_MODEL // 2


def _pack_words(rows):
    lo = lax.shift_right_logical(lax.bitcast_convert_type(rows[:, :HALF], jnp.uint32), jnp.uint32(16))
    hi = lax.bitcast_convert_type(rows[:, HALF:], jnp.uint32) & jnp.uint32(0xFFFF0000)
    return lo | hi


def _unpack_words(words):
    lo = lax.bitcast_convert_type(lax.shift_left(words, jnp.uint32(16)), F32)
    hi = lax.bitcast_convert_type(words & jnp.uint32(0xFFFF0000), F32)
    return jnp.concatenate([lo, hi], axis=1).astype(BF16)


def _route_kernel(h_ref, rt_ref, gate_ref, rank_ref, cnt_ref):
    tb = h_ref.shape[0]
    logits = jnp.dot(h_ref[...], rt_ref[...], preferred_element_type=F32)
    gate, sel = _top2_gate(logits)
    gate_ref[...] = gate
    r = lax.broadcasted_iota(jnp.int32, (tb, tb), 0)
    c = lax.broadcasted_iota(jnp.int32, (tb, tb), 1)
    before = jnp.where(c < r, 1.0, 0.0).astype(BF16)
    rank = jnp.dot(before, sel.astype(BF16), preferred_element_type=F32)
    rank_ref[...] = jnp.where(sel > 0.5, rank, -1.0).T
    cnt_ref[...] = jnp.sum(sel, axis=0, keepdims=True)


def _route_call(h2, router_p, tb):
    t, d = h2.shape
    nb = t // tb
    return pl.pallas_call(
        _route_kernel,
        grid=(nb,),
        in_specs=[pl.BlockSpec((tb, d), lambda i: (i, 0)), pl.BlockSpec((d, LANES), lambda i: (0, 0))],
        out_specs=[pl.BlockSpec((tb, LANES), lambda i: (i, 0)),
                   pl.BlockSpec((None, LANES, tb), lambda i: (i, 0, 0)),
                   pl.BlockSpec((None, 1, LANES), lambda i: (i, 0, 0))],
        out_shape=[jax.ShapeDtypeStruct((t, LANES), F32),
                   jax.ShapeDtypeStruct((nb, LANES, tb), F32),
                   jax.ShapeDtypeStruct((nb, 1, LANES), F32)],
        compiler_params=_cparams(("arbitrary",)),
        name="route",
    )(h2, router_p)


def _group_windows(cnt_ref, b):
    return lambda e: (cnt_ref[b * N_EXPERTS + e] + (MOE_WIN - 1)) // MOE_WIN


def _onehot_rows(rank_row, w, tb):
    slot = lax.broadcasted_iota(jnp.int32, (MOE_WIN, tb), 0).astype(F32) + (w * MOE_WIN).astype(F32)
    return jnp.where(rank_row == slot, 1.0, 0.0).astype(BF16)


def _pack_kernel(start_ref, cnt_ref, h_ref, rank_ref, hs_in_ref, hs_ref, buf, sem):
    del hs_in_ref
    b = pl.program_id(0)
    tb = h_ref.shape[0]
    nw = tb // MOE_WIN
    windows = _group_windows(cnt_ref, b)

    def copy(e, w):
        row0 = pl.multiple_of(start_ref[b * N_EXPERTS + e] + w * MOE_WIN, SUBLANES)
        return pltpu.make_async_copy(buf.at[e * nw + w], hs_ref.at[pl.ds(row0, MOE_WIN)], sem.at[e * nw + w])

    def pack_group(e, carry):
        rank_row = rank_ref[pl.ds(e, 1), :]

        def pack_window(w, c):
            rows = jnp.dot(_onehot_rows(rank_row, w, tb), h_ref[...], preferred_element_type=F32)
            buf[e * nw + w] = _pack_words(rows)
            copy(e, w).start()
            return c

        return lax.fori_loop(0, windows(e), pack_window, carry)

    lax.fori_loop(0, N_EXPERTS, pack_group, 0)

    def wait_group(e, carry):
        def wait_window(w, c):
            copy(e, w).wait()
            return c

        return lax.fori_loop(0, windows(e), wait_window, carry)

    lax.fori_loop(0, N_EXPERTS, wait_group, 0)


def _pack_call(start, cnt, h2, rank_t, hs_zero, tb):
    t, d = h2.shape
    nw = tb // MOE_WIN
    return pl.pallas_call(
        _pack_kernel,
        grid_spec=pltpu.PrefetchScalarGridSpec(
            num_scalar_prefetch=2,
            grid=(t // tb,),
            in_specs=[pl.BlockSpec((tb, d), lambda i, s, c: (i, 0)),
                      pl.BlockSpec((None, LANES, tb), lambda i, s, c: (i, 0, 0)),
                      pl.BlockSpec(memory_space=pl.ANY)],
            out_specs=pl.BlockSpec(memory_space=pl.ANY),
            scratch_shapes=[pltpu.VMEM((N_EXPERTS * nw, MOE_WIN, HALF), jnp.uint32),
                            pltpu.SemaphoreType.DMA((N_EXPERTS * nw,))]),
        out_shape=jax.ShapeDtypeStruct(hs_zero.shape, jnp.uint32),
        input_output_aliases={4: 0},
        compiler_params=_cparams(("arbitrary",)),
        name="pack",
    )(start, cnt, h2, rank_t, hs_zero)


def _expert_kernel(te_ref, nu_ref, hs_ref, w1_ref, w3_ref, w2_ref, ys_ref):
    del te_ref
    i = pl.program_id(0)

    @pl.when(i < nu_ref[0])
    def _():
        y = _swiglu(_unpack_words(hs_ref[...]), w1_ref, w3_ref, w2_ref)
        ys_ref[...] = _pack_words(y.astype(BF16).astype(F32))

    @pl.when(i >= nu_ref[0])
    def _():
        ys_ref[...] = jnp.zeros_like(ys_ref)


def _expert_call(tile_expert, n_used, hs, w1, w3, w2):
    r, half = hs.shape
    _, d, ff = w1.shape
    row = pl.BlockSpec((MOE_TG, half), lambda i, te, nu: (i, 0))
    return pl.pallas_call(
        _expert_kernel,
        grid_spec=pltpu.PrefetchScalarGridSpec(
            num_scalar_prefetch=2,
            grid=(r // MOE_TG,),
            in_specs=[row,
                      pl.BlockSpec((None, d, ff), lambda i, te, nu: (te[i], 0, 0)),
                      pl.BlockSpec((None, d, ff), lambda i, te, nu: (te[i], 0, 0)),
                      pl.BlockSpec((None, ff, d), lambda i, te, nu: (te[i], 0, 0))],
            out_specs=row),
        out_shape=jax.ShapeDtypeStruct((r, half), jnp.uint32),
        compiler_params=_cparams(("arbitrary",)),
        name="experts",
    )(tile_expert, n_used, hs, w1, w3, w2)


def _combine_kernel(start_ref, cnt_ref, gate_ref, rank_ref, x_ref, mgate_ref, gpost_ref, ys_ref,
                    xo_ref, buf, acc_scr, sem):
    b = pl.program_id(0)
    tb = x_ref.shape[0]
    nw = tb // MOE_WIN
    windows = _group_windows(cnt_ref, b)

    def copy(e, w):
        row0 = pl.multiple_of(start_ref[b * N_EXPERTS + e] + w * MOE_WIN, SUBLANES)
        return pltpu.make_async_copy(ys_ref.at[pl.ds(row0, MOE_WIN)], buf.at[e * nw + w], sem.at[e * nw + w])

    def fetch_group(e, carry):
        def fetch_window(w, c):
            copy(e, w).start()
            return c

        return lax.fori_loop(0, windows(e), fetch_window, carry)

    lax.fori_loop(0, N_EXPERTS, fetch_group, 0)
    acc_scr[...] = jnp.zeros_like(acc_scr)
    lane = lax.broadcasted_iota(jnp.int32, gate_ref.shape, 1)

    def expand_group(e, carry):
        rank_row = rank_ref[pl.ds(e, 1), :]
        gcol = jnp.sum(jnp.where(lane == e, gate_ref[...], 0.0), axis=-1, keepdims=True)

        def expand_window(w, c):
            copy(e, w).wait()
            live = lax.broadcasted_iota(jnp.int32, (MOE_WIN, 1), 0) < cnt_ref[b * N_EXPERTS + e] - w * MOE_WIN
            y = _unpack_words(jnp.where(live, buf[e * nw + w], jnp.uint32(0)))
            back = lax.dot_general(_onehot_rows(rank_row, w, tb), y, (((0,), (0,)), ((), ())),
                                   preferred_element_type=F32)
            acc_scr[...] += gcol * back
            return c

        return lax.fori_loop(0, windows(e), expand_window, carry)

    lax.fori_loop(0, N_EXPERTS, expand_group, 0)
    xo_ref[...] = x_ref[...] + mgate_ref[...] * _rms(acc_scr[...], gpost_ref[...])


def _combine_call(start, cnt, gate, rank_t, x2, mod3, g_post, ys, rows_per_mod, tb):
    t, d = x2.shape
    nw = tb // MOE_WIN
    row = pl.BlockSpec((tb, d), lambda i, s, c: (i, 0))
    return pl.pallas_call(
        _combine_kernel,
        grid_spec=pltpu.PrefetchScalarGridSpec(
            num_scalar_prefetch=2,
            grid=(t // tb,),
            in_specs=[pl.BlockSpec((tb, LANES), lambda i, s, c: (i, 0)),
                      pl.BlockSpec((None, LANES, tb), lambda i, s, c: (i, 0, 0)),
                      row,
                      pl.BlockSpec((None, 1, d), lambda i, s, c: (i // rows_per_mod, 0, 5)),
                      pl.BlockSpec((1, d), lambda i, s, c: (0, 0)),
                      pl.BlockSpec(memory_space=pl.ANY)],
            out_specs=row,
            scratch_shapes=[pltpu.VMEM((N_EXPERTS * nw, MOE_WIN, HALF), jnp.uint32),
                            pltpu.VMEM((tb, d), F32),
                            pltpu.SemaphoreType.DMA((N_EXPERTS * nw,))]),
        out_shape=jax.ShapeDtypeStruct((t, d), F32),
        compiler_params=_cparams(("arbitrary",)),
        name="combine",
    )(start, cnt, gate, rank_t, x2, mod3, g_post.reshape(1, d), ys)


def _moe_sorted(h2, x2, mod3, g_post, router_p, w1, w3, w2, rows_per_mod, tb):
    t, d = h2.shape
    nb = t // tb
    gate, rank_t, cnt3 = _route_call(h2, router_p, tb)
    cnt = cnt3[:, 0, :N_EXPERTS].astype(jnp.int32)
    seg = (cnt + SUBLANES - 1) // SUBLANES * SUBLANES
    cap = (seg.sum(axis=0) + MOE_WIN + MOE_TG - 1) // MOE_TG * MOE_TG
    ends = jnp.cumsum(cap)
    start = ((ends - cap)[None, :] + jnp.cumsum(seg, axis=0) - seg).reshape(-1)
    r_max = (TOP_K * t + N_EXPERTS * (nb * SUBLANES + MOE_WIN + MOE_TG)) // MOE_TG * MOE_TG
    tile_row = jnp.arange(r_max // MOE_TG, dtype=jnp.int32) * MOE_TG
    tile_expert = jnp.minimum(jnp.sum(tile_row[:, None] >= ends[None, :], axis=1), N_EXPERTS - 1).astype(jnp.int32)
    n_used = (ends[-1:] // MOE_TG).astype(jnp.int32)
    hs = _pack_call(start, cnt.reshape(-1), h2, rank_t, jnp.zeros((r_max, HALF), jnp.uint32), tb)
    ys = _expert_call(tile_expert, n_used, hs, w1, w3, w2)
    return _combine_call(start, cnt.reshape(-1), gate, rank_t, x2, mod3, g_post, ys, rows_per_mod, tb)


def _prep_w_in(w):
    o_q, o_k, o_v, o_c, o_gq, o_gk, o_gv, o_gg, o_ga, o_gt = 0, 512, 1024, 1536, 2560, 2816, 3072, 3584, 4096, 4128
    perm = (jnp.arange(DA_WIDTH // LANES)[:, None] * LANES + _qk_lane_perm()[None, :]).reshape(-1)
    main = jnp.concatenate([w[:, o_q:o_k][:, perm], w[:, o_k:o_v][:, perm], w[:, o_v:o_c], w[:, o_gv:o_gg], w[:, o_c:o_gq],
                            w[:, o_gg:o_ga], w[:, o_gq:o_gk], w[:, o_gk:o_gv], w[:, o_gt:]], axis=1)
    ga = jnp.pad(w[:, o_ga:o_gt], ((0, 0), (0, LANES - 2 * GLA_RANK)))
    return main.astype(BF16), ga.astype(BF16)


def _prep_a_up(a_up, a_b):
    m = jnp.zeros((LANES, 2 * GLA_KW), F32)
    m = m.at[:GLA_RANK, :GLA_KW].set(a_up[0]).at[GLA_RANK:2 * GLA_RANK, GLA_KW:].set(a_up[1])
    return m.astype(BF16), a_b.reshape(1, 2 * GLA_KW).astype(F32)


def _row_tile(n, pref):
    t = min(n, pref)
    assert n % t == 0
    return t


def kernel(x, c, ctx, c_ctx, ada_w, ada_b, g_mix_pre, g_mix_post, g_ffn_pre, g_ffn_post, w_in, da_lambda, da_subln, da_proj, cv_dw, cv_dw_b, cv_ln_g, cv_ln_b, cv_proj, gla_a_up, gla_a_b, gla_norm, gla_proj, w_out, ffn_w1, ffn_w3, ffn_w2, moe_router, moe_w1, moe_w3, moe_w2):
    bsz, seq, d = x.shape
    lc = ctx.shape[1]
    depth = w_in.shape[0]
    assert bsz + 1 <= MOD_ROWS and d == D_MODEL
    t_lat, t_ctx = bsz * seq, bsz * lc

    cc = jnp.zeros((MOD_ROWS, d), F32).at[:bsz].set(c).at[bsz].set(c_ctx)
    mod_all = _mod_call(cc, ada_w, ada_b)
    rope_tabs = _rope_tables(seq)

    tm_lat = _row_tile(seq, 2048)
    tm_ctx = _row_tile(t_ctx, 1024)
    tm_mix = _row_tile(seq, 512)
    tm_mix_ctx = _row_tile(t_ctx, 512)
    tq = _row_tile(seq, 512)
    tq_ctx = _row_tile(lc, 128)
    tc = _row_tile(seq, 512)
    tc_ctx = _row_tile(lc, 256)

    xl = x.reshape(t_lat, d)
    xc = ctx.reshape(t_ctx, d)
    for l in range(depth):
        need_ctx = l < depth - 1
        lambda_init = 0.8 - 0.6 * math.exp(-0.3 * l)
        mod3 = mod_all[l].reshape(MOD_ROWS, 1, 6 * d)
        w_main, w_ga = _prep_w_in(w_in[l])
        aup, ab = _prep_a_up(gla_a_up[l], gla_a_b[l])
        wa, wb, wc, wo = (da_proj[l].astype(BF16), cv_proj[l].astype(BF16),
                          gla_proj[l].astype(BF16), w_out[l].astype(BF16))

        z, ga = _inproj_call(xl, g_mix_pre[l], mod3, w_main, w_ga, rope_tabs, seq // tm_lat, 0, tm_lat)
        zc, gac = _inproj_call(xc, g_mix_pre[l], mod3, w_main, w_ga, None, t_ctx, bsz, tm_ctx)
        z3, zc3 = z.reshape(bsz, seq, Z_WIDTH), zc.reshape(bsz, lc, Z_WIDTH)

        ya = _attn_call(da_lambda[l], da_subln[l], z3, [z3, zc3], lambda_init, tq)
        yb = _conv_call(z3, cv_dw[l], cv_dw_b[l], cv_ln_g[l], cv_ln_b[l], tc)
        s_zero = jnp.zeros((bsz, 2, GLA_HEADS * GLA_DK, GLA_DV), F32)
        ycc, s_ctx = _gla_call(zc3, gac.reshape(bsz, lc, LANES), aup, ab, gla_norm[l], s_zero)
        yc, _ = _gla_call(z3, ga.reshape(bsz, seq, LANES), aup, ab, gla_norm[l], s_ctx)

        xl, h2 = _merge_call(ya.reshape(t_lat, -1), yb.reshape(t_lat, -1), yc.reshape(t_lat, -1), z, xl, mod3,
                             g_mix_post[l], g_ffn_pre[l], wa, wb, wc, wo, seq // tm_mix, 0, tm_mix)
        if need_ctx:
            yac = _attn_call(da_lambda[l], da_subln[l], zc3, [zc3], lambda_init, tq_ctx)
            ybc = _conv_call(zc3, cv_dw[l], cv_dw_b[l], cv_ln_g[l], cv_ln_b[l], tc_ctx)
            xc, hc2 = _merge_call(yac.reshape(t_ctx, -1), ybc.reshape(t_ctx, -1), ycc.reshape(t_ctx, -1), zc, xc,
                                  mod3, g_mix_post[l], g_ffn_pre[l], wa, wb, wc, wo, t_ctx, bsz, tm_mix_ctx)

        i = l // 2
        if l % 2 == 0:
            w1, w3, w2 = ffn_w1[i].astype(BF16), ffn_w3[i].astype(BF16), ffn_w2[i].astype(BF16)
            xl = _ffn_call(h2, xl, mod3, g_ffn_post[l], w1, w3, w2, seq // tm_mix, 0, tm_mix)
            if need_ctx:
                xc = _ffn_call(hc2, xc, mod3, g_ffn_post[l], w1, w3, w2, t_ctx, bsz, tm_mix_ctx)
        else:
            router_p = jnp.pad(moe_router[i], ((0, 0), (0, LANES - N_EXPERTS))).astype(BF16)
            w1, w3, w2 = moe_w1[i].astype(BF16), moe_w3[i].astype(BF16), moe_w2[i].astype(BF16)
            if seq % MOE_TB == 0:
                xl = _moe_sorted(h2, xl, mod3, g_ffn_post[l], router_p, w1, w3, w2, seq // MOE_TB, MOE_TB)
            else:
                f = _moe_call(h2, router_p, w1, w3, w2, _row_tile(t_lat, 1024))
                xl = _resid_call(f, xl, mod3, g_ffn_post[l], seq // tm_mix, 0, tm_mix)
            if need_ctx:
                fc = _moe_call(hc2, router_p, w1, w3, w2, _row_tile(t_ctx, 1024))
                xc = _resid_call(fc, xc, mod3, g_ffn_post[l], t_ctx, bsz, tm_mix_ctx)
    return xl.reshape(bsz, seq, d)
```
